```python
import functools
import jax
import jax.numpy as jnp
from jax import lax
import numpy as np

D_MODEL = 1024
BATCH = 4
SEQ = 8192
DEPTH = 1
DEC_BATCH = 128
DEC_SEQ = 4
PAST_LEN = 8192
PAGE_SIZE = 128

D_ATT = D_MODEL // 2
D_CONV = D_MODEL - D_ATT
D_MIX = D_ATT + D_CONV
N_HEADS = 8
HEAD_DIM = D_ATT // N_HEADS
CONV_WIDTH = 31
D_PLE = 256
Q_BLOCK = 128
EPS = 1e-6
ATT_SCALE = HEAD_DIM ** -0.5
D_IN = 4 * D_ATT + N_HEADS + 3 * D_CONV
SPLITS = (D_ATT, 2 * D_ATT, 3 * D_ATT, 3 * D_ATT + N_HEADS, 4 * D_ATT + N_HEADS,
          4 * D_ATT + N_HEADS + D_CONV, 4 * D_ATT + N_HEADS + 2 * D_CONV)

kernel_name = "hymba_fox_conformer_conv_step"


def _rmsnorm(x, g):
    xf = x.astype(jnp.float32)
    y = xf * lax.rsqrt(jnp.mean(xf * xf, axis=-1, keepdims=True) + EPS)
    return (y * g.astype(jnp.float32)).astype(x.dtype)


def _layernorm(x, g, b):
    xf = x.astype(jnp.float32)
    mu = jnp.mean(xf, axis=-1, keepdims=True)
    var = jnp.mean(jnp.square(xf - mu), axis=-1, keepdims=True)
    y = (xf - mu) * lax.rsqrt(var + EPS)
    return (y * g.astype(jnp.float32) + b.astype(jnp.float32)).astype(x.dtype)


def _fox_block(q, k, v, c_q, c_k, q_pos, k_pos):
    s = jnp.einsum('bqhd,bkhd->bhqk', q, k, preferred_element_type=jnp.float32) * ATT_SCALE
    bias = jnp.transpose(c_q, (0, 2, 1))[..., :, None] - jnp.transpose(c_k, (0, 2, 1))[..., None, :]
    causal = k_pos[None, :] <= q_pos[:, None]
    logits = jnp.where(causal[None, None], s + bias, -jnp.inf)
    p = jax.nn.softmax(logits, axis=-1)
    return jnp.einsum('bhqk,bkhd->bqhd', p.astype(v.dtype), v)


def _attend_prompt(q, k, v, logf):
    b, t = q.shape[:2]
    nb = t // Q_BLOCK
    c = jnp.cumsum(logf, axis=1)
    qb = jnp.transpose(q.reshape(b, nb, Q_BLOCK, N_HEADS, HEAD_DIM), (1, 0, 2, 3, 4))
    cqb = jnp.transpose(c.reshape(b, nb, Q_BLOCK, N_HEADS), (1, 0, 2, 3))
    posb = jnp.arange(t, dtype=jnp.int32).reshape(nb, Q_BLOCK)
    k_pos = jnp.arange(t, dtype=jnp.int32)
    out = lax.map(lambda a: _fox_block(a[0], k, v, a[1], c, a[2], k_pos), (qb, cqb, posb))
    return jnp.transpose(out, (1, 0, 2, 3, 4)).reshape(b, t, D_ATT)


def _attend_sample(q, k, v, logf, cache_k, cache_v, cache_logf, page_table):
    db, t = q.shape[:2]
    past = page_table.shape[1] * cache_k.shape[1]
    k_past = cache_k[page_table].reshape(db, past, N_HEADS, HEAD_DIM)
    v_past = cache_v[page_table].reshape(db, past, N_HEADS, HEAD_DIM)
    f_past = cache_logf[page_table].reshape(db, past, N_HEADS).astype(jnp.float32)
    k_all = jnp.concatenate([k_past, k.astype(k_past.dtype)], axis=1)
    v_all = jnp.concatenate([v_past, v.astype(v_past.dtype)], axis=1)
    c = jnp.cumsum(jnp.concatenate([f_past, logf], axis=1), axis=1)
    q_pos = past + jnp.arange(t, dtype=jnp.int32)
    k_pos = jnp.arange(past + t, dtype=jnp.int32)
    out = _fox_block(q, k_all, v_all, c[:, past:], c, q_pos, k_pos)
    return out.reshape(db, t, D_ATT)


def _conv_branch(u, prefix, w_dw, b_dw, ln_g, ln_b):
    up = jnp.concatenate([prefix.astype(u.dtype), u], axis=1)
    y = lax.conv_general_dilated(up, w_dw[:, None, :], window_strides=(1,), padding='VALID',
                                 dimension_numbers=('NWC', 'WIO', 'NWC'),
                                 feature_group_count=D_CONV) + b_dw
    y = jax.nn.silu(_layernorm(y, ln_g, ln_b))
    return y, up[:, -(CONV_WIDTH - 1):]


def _layer(x, p_l, conv_prefix, attend, g_norm, w_in, b_f, w_dw, b_dw, ln_g, ln_b, w_out, w_pe, g_pe, w_pg):
    b, t = x.shape[:2]
    xn = _rmsnorm(x, g_norm)
    z = xn @ w_in
    q, k, v, fz, ga, glu_a, glu_b, gc = jnp.split(z, SPLITS, axis=-1)
    q = q.reshape(b, t, N_HEADS, HEAD_DIM)
    k = k.reshape(b, t, N_HEADS, HEAD_DIM)
    v = v.reshape(b, t, N_HEADS, HEAD_DIM)
    logf = jax.nn.log_sigmoid((fz + b_f).astype(jnp.float32))
    att = attend(q, k, v, logf)
    u = glu_a * jax.nn.sigmoid(glu_b)
    cv, new_buf = _conv_branch(u, conv_prefix, w_dw, b_dw, ln_g, ln_b)
    mix = jnp.concatenate([att.astype(x.dtype) * jax.nn.silu(ga), cv * jax.nn.silu(gc)], axis=-1)
    h = x + mix @ w_out
    e = _rmsnorm(p_l.astype(x.dtype) @ w_pe, g_pe)
    h = h + jax.nn.sigmoid(h @ w_pg) * e
    return h, k, v, logf, new_buf


def setup_inputs(seed: int = 0) -> dict:
    key = jax.random.key(seed)
    ks = jax.random.split(key, 24)
    f32 = jnp.float32
    n_pages = PAST_LEN // PAGE_SIZE
    n_used = DEC_BATCH * n_pages
    n_phys = n_used + n_used // 4
    nrm = jax.random.normal
    x_prompt = nrm(ks[0], (BATCH, SEQ, D_MODEL), f32)
    x_sample = nrm(ks[1], (DEC_BATCH, DEC_SEQ, D_MODEL), f32)
    cache_k = nrm(ks[2], (DEPTH, n_phys, PAGE_SIZE, N_HEADS, HEAD_DIM), f32)
    cache_v = nrm(ks[3], (DEPTH, n_phys, PAGE_SIZE, N_HEADS, HEAD_DIM), f32)
    cache_logf = jax.nn.log_sigmoid(nrm(ks[4], (DEPTH, n_phys, PAGE_SIZE, N_HEADS), f32) + 3.0)
    state_conv = 0.5 * nrm(ks[5], (DEPTH, DEC_BATCH, CONV_WIDTH - 1, D_CONV), f32)
    page_table = jax.random.permutation(ks[6], n_phys)[:n_used].reshape(DEC_BATCH, n_pages).astype(jnp.int32)
    p_prompt = nrm(ks[7], (DEPTH, BATCH, SEQ, D_PLE), f32)
    p_sample = nrm(ks[8], (DEPTH, DEC_BATCH, DEC_SEQ, D_PLE), f32)
    g_norm = 1.0 + 0.1 * nrm(ks[9], (DEPTH, D_MODEL), f32)
    w_in = nrm(ks[10], (DEPTH, D_MODEL, D_IN), f32) * D_MODEL ** -0.5
    b_f = jax.random.uniform(ks[11], (DEPTH, N_HEADS), f32, minval=1.0, maxval=5.0)
    w_dw = nrm(ks[12], (DEPTH, CONV_WIDTH, D_CONV), f32) * CONV_WIDTH ** -0.5
    b_dw = 0.02 * nrm(ks[13], (DEPTH, D_CONV), f32)
    ln_g = 1.0 + 0.1 * nrm(ks[14], (DEPTH, D_CONV), f32)
    ln_b = 0.02 * nrm(ks[15], (DEPTH, D_CONV), f32)
    w_out = nrm(ks[16], (DEPTH, D_MIX, D_MODEL), f32) * D_MIX ** -0.5
    w_pe = nrm(ks[17], (DEPTH, D_PLE, D_MODEL), f32) * D_PLE ** -0.5
    g_pe = 1.0 + 0.1 * nrm(ks[18], (DEPTH, D_MODEL), f32)
    w_pg = nrm(ks[19], (DEPTH, D_MODEL, D_MODEL), f32) * D_MODEL ** -0.5
    g_final = 1.0 + 0.1 * nrm(ks[20], (D_MODEL,), f32)
    return {"x_prompt": x_prompt, "x_sample": x_sample, "cache_k": cache_k, "cache_v": cache_v,
            "cache_logf": cache_logf, "state_conv": state_conv, "page_table": page_table,
            "p_prompt": p_prompt, "p_sample": p_sample, "g_norm": g_norm, "w_in": w_in, "b_f": b_f,
            "w_dw": w_dw, "b_dw": b_dw, "ln_g": ln_g, "ln_b": ln_b, "w_out": w_out, "w_pe": w_pe,
            "g_pe": g_pe, "w_pg": w_pg, "g_final": g_final}


def reference(x_prompt, x_sample, cache_k, cache_v, cache_logf, state_conv, page_table, p_prompt, p_sample,
              g_norm, w_in, b_f, w_dw, b_dw, ln_g, ln_b, w_out, w_pe, g_pe, w_pg, g_final):
    y_p, y_s = x_prompt, x_sample
    kp_l, vp_l, fp_l, cp_l = [], [], [], []
    ks_l, vs_l, fs_l, cs_l = [], [], [], []
    for l in range(DEPTH):
        params = (g_norm[l], w_in[l], b_f[l], w_dw[l], b_dw[l], ln_g[l], ln_b[l], w_out[l], w_pe[l], g_pe[l], w_pg[l])
        zero_prefix = jnp.zeros((y_p.shape[0], CONV_WIDTH - 1, D_CONV), y_p.dtype)
        y_p, kp, vp, fp, cp = _layer(y_p, p_prompt[l], zero_prefix, _attend_prompt, *params)
        attend_s = functools.partial(_attend_sample, cache_k=cache_k[l], cache_v=cache_v[l],
                                     cache_logf=cache_logf[l], page_table=page_table)
        y_s, ksm, vsm, fsm, csm = _layer(y_s, p_sample[l], state_conv[l], attend_s, *params)
        kp_l.append(kp); vp_l.append(vp); fp_l.append(fp); cp_l.append(cp)
        ks_l.append(ksm); vs_l.append(vsm); fs_l.append(fsm); cs_l.append(csm)
    y_prompt = _rmsnorm(y_p, g_final)
    y_sample = _rmsnorm(y_s, g_final)
    k_prompt = jnp.stack(kp_l); v_prompt = jnp.stack(vp_l)
    logf_prompt = jnp.stack(fp_l); conv_prompt = jnp.stack(cp_l)
    k_sample = jnp.stack(ks_l); v_sample = jnp.stack(vs_l)
    logf_sample = jnp.stack(fs_l); conv_sample = jnp.stack(cs_l)
    return (y_prompt, y_sample, k_prompt, v_prompt, logf_prompt, conv_prompt,
            k_sample, v_sample, logf_sample, conv_sample)
```

```python
import functools

import numpy as np
import jax
import jax.numpy as jnp
from jax import lax
from jax.experimental import pallas as pl
from jax.experimental.pallas import tpu as pltpu

D_MODEL = 1024
D_ATT = 512
D_CONV = 512
N_HEADS = 8
HEAD_DIM = 64
CONV_WIDTH = 31
CONV_STATE = CONV_WIDTH - 1
D_PLE = 256
PAGE_SIZE = 128
EPS = 1e-6
ATT_SCALE = HEAD_DIM ** -0.5
NEG_BIG = -1e30

LANES = 128
SUBLANES = 8
BF16_SUBLANES = 16
F_PAD = LANES

VMEM_LIMIT = 56 * 1024 * 1024

F32 = jnp.float32
BF16 = jnp.bfloat16
NT_DIMS = (((1,), (1,)), ((), ()))


def _silu(x):
    return x * jax.nn.sigmoid(x)


def _cparams(sem):
    return pltpu.CompilerParams(dimension_semantics=sem, vmem_limit_bytes=VMEM_LIMIT)


def _rms_normed(x_ref, g_ref):
    x = x_ref[...]
    ms = jnp.mean(x * x, axis=-1, keepdims=True)
    return ((x * lax.rsqrt(ms + EPS)) * g_ref[...]).astype(BF16)


PW_Q, PW_V, PW_GA, PW_GLUA, PW_GLUB, PW_GC = 0, 512, 1024, 1536, 2048, 2560
PW_COLS = 3072


def _proj_prompt_kernel(x_ref, g_ref, w_ref, wkt_ref, wft_ref, bf_ref,
                        qb_ref, vb_ref, vf_ref, ktb_ref, ktf_ref, logft_ref, ct_ref, sga_ref, u_ref, sgc_ref,
                        carry_ref, *, tm):
    xn = _rms_normed(x_ref, g_ref)

    def proj(lo, width):
        return jnp.dot(xn, w_ref[:, lo:lo + width], preferred_element_type=F32)

    qb_ref[...] = (proj(PW_Q, D_ATT) * ATT_SCALE).astype(BF16)
    v = proj(PW_V, D_ATT)
    vf_ref[...] = v
    vb_ref[...] = v.astype(BF16)
    sga_ref[...] = _silu(proj(PW_GA, D_ATT))
    u_ref[...] = proj(PW_GLUA, D_CONV) * jax.nn.sigmoid(proj(PW_GLUB, D_CONV))
    sgc_ref[...] = _silu(proj(PW_GC, D_CONV))

    kt = lax.dot_general(wkt_ref[...], xn, NT_DIMS, preferred_element_type=F32)
    ktf_ref[...] = kt
    ktb_ref[...] = kt.astype(BF16)
    fzt = lax.dot_general(wft_ref[...], xn, NT_DIMS, preferred_element_type=F32)
    logft = jax.nn.log_sigmoid(fzt + bf_ref[...])[:N_HEADS, :]
    logft_ref[...] = logft

    @pl.when(pl.program_id(1) == 0)
    def _():
        carry_ref[...] = jnp.zeros_like(carry_ref)

    lane = lax.broadcasted_iota(jnp.int32, (N_HEADS, tm), 1)
    c = logft
    shift = 1
    while shift < tm:
        c = c + jnp.where(lane >= shift, pltpu.roll(c, shift, axis=1), 0.0)
        shift *= 2
    c = c + carry_ref[:, 0:1]
    ct_ref[...] = c
    carry_ref[...] = jnp.broadcast_to(c[:, tm - 1:tm], carry_ref.shape)


def _proj_prompt(x, g_norm, w_tok, wkt, wft, bf_col, *, tm=512):
    b, t, _ = x.shape
    tok = lambda d: pl.BlockSpec((None, tm, d), lambda i, j: (i, j, 0))
    tr = lambda r: pl.BlockSpec((None, r, tm), lambda i, j: (i, 0, j))
    const = lambda shape: pl.BlockSpec(shape, lambda i, j: (0,) * len(shape))
    out_shape = (
        jax.ShapeDtypeStruct((b, t, D_ATT), BF16),
        jax.ShapeDtypeStruct((b, t, D_ATT), BF16),
        jax.ShapeDtypeStruct((b, t, D_ATT), F32),
        jax.ShapeDtypeStruct((b, D_ATT, t), BF16),
        jax.ShapeDtypeStruct((b, D_ATT, t), F32),
        jax.ShapeDtypeStruct((b, N_HEADS, t), F32),
        jax.ShapeDtypeStruct((b, N_HEADS, t), F32),
        jax.ShapeDtypeStruct((b, t, D_ATT), F32),
        jax.ShapeDtypeStruct((b, t, D_CONV), F32),
        jax.ShapeDtypeStruct((b, t, D_CONV), F32),
    )
    out_specs = (tok(D_ATT), tok(D_ATT), tok(D_ATT), tr(D_ATT), tr(D_ATT), tr(N_HEADS), tr(N_HEADS),
                 tok(D_ATT), tok(D_CONV), tok(D_CONV))
    return pl.pallas_call(
        functools.partial(_proj_prompt_kernel, tm=tm),
        grid=(b, t // tm),
        in_specs=[tok(D_MODEL), const((1, D_MODEL)), const(w_tok.shape), const(wkt.shape), const(wft.shape),
                  const(bf_col.shape)],
        out_specs=out_specs,
        out_shape=out_shape,
        scratch_shapes=[pltpu.VMEM((N_HEADS, LANES), F32)],
        compiler_params=_cparams(("arbitrary", "arbitrary")),
        name="proj_prompt",
    )(x, g_norm, w_tok, wkt, wft, bf_col)


SW_Q, SW_K, SW_V, SW_GA, SW_GLUA, SW_GLUB, SW_GC, SW_F = 0, 512, 1024, 1536, 2048, 2560, 3072, 3584
SW_COLS = SW_F + F_PAD


def _proj_sample_kernel(x_ref, g_ref, w_ref, bf_ref, qb_ref, k_ref, v_ref, logf_ref, sga_ref, u_ref, sgc_ref):
    xn = _rms_normed(x_ref, g_ref)

    def proj(lo, width):
        return jnp.dot(xn, w_ref[:, lo:lo + width], preferred_element_type=F32)

    qb_ref[...] = (proj(SW_Q, D_ATT) * ATT_SCALE).astype(BF16)
    k_ref[...] = proj(SW_K, D_ATT)
    v_ref[...] = proj(SW_V, D_ATT)
    sga_ref[...] = _silu(proj(SW_GA, D_ATT))
    u_ref[...] = proj(SW_GLUA, D_CONV) * jax.nn.sigmoid(proj(SW_GLUB, D_CONV))
    sgc_ref[...] = _silu(proj(SW_GC, D_CONV))
    logf_ref[...] = jax.nn.log_sigmoid(proj(SW_F, F_PAD) + bf_ref[...])[:, :N_HEADS]


def _proj_sample(x, g_norm, w_tok, bf_row):
    n = x.shape[0]
    full = lambda shape: pl.BlockSpec(shape, lambda i: (0,) * len(shape))
    wide = jax.ShapeDtypeStruct((n, D_ATT), F32)
    out_shape = (jax.ShapeDtypeStruct((n, D_ATT), BF16), wide, wide, jax.ShapeDtypeStruct((n, N_HEADS), F32),
                 wide, wide, wide)
    return pl.pallas_call(
        _proj_sample_kernel,
        grid=(1,),
        in_specs=[full(x.shape), full(g_norm.shape), full(w_tok.shape), full(bf_row.shape)],
        out_specs=tuple(full(s.shape) for s in out_shape),
        out_shape=out_shape,
        compiler_params=_cparams(("arbitrary",)),
        name="proj_sample",
    )(x, g_norm, w_tok, bf_row)


CONV_HALO = 32


def _ln_swish(y, lg_ref, lb_ref):
    mu = jnp.mean(y, axis=-1, keepdims=True)
    yc = y - mu
    var = jnp.mean(yc * yc, axis=-1, keepdims=True)
    return _silu(yc * lax.rsqrt(var + EPS) * lg_ref[...] + lb_ref[...])


def _conv_prompt_kernel(u_ref, sgc_ref, w_ref, b_ref, lg_ref, lb_ref, o_ref, ext_ref, *, tc, rc):
    @pl.when(pl.program_id(1) == 0)
    def _():
        ext_ref[0:CONV_HALO, :] = jnp.zeros((CONV_HALO, D_CONV), F32)

    ext_ref[CONV_HALO:CONV_HALO + tc, :] = u_ref[...]
    off = CONV_HALO - CONV_STATE

    def chunk(ci, carry):
        r0 = pl.multiple_of(ci * rc, rc)
        win = ext_ref[pl.ds(r0, rc + CONV_HALO), :]
        acc = jnp.zeros((rc, D_CONV), F32)
        for j in range(CONV_WIDTH):
            acc = acc + win[off + j:off + j + rc, :] * w_ref[j:j + 1, :]
        o_ref[pl.ds(r0, rc), :] = _ln_swish(acc + b_ref[...], lg_ref, lb_ref) * sgc_ref[pl.ds(r0, rc), :]
        return carry

    lax.fori_loop(0, tc // rc, chunk, 0)
    ext_ref[0:CONV_HALO, :] = ext_ref[tc:tc + CONV_HALO, :]


def _conv_prompt(u, sgc, w_dw, b_dw, ln_g, ln_b, *, tc=512, rc=32):
    b, t, _ = u.shape
    tok = pl.BlockSpec((None, tc, D_CONV), lambda i, j: (i, j, 0))
    const = lambda shape: pl.BlockSpec(shape, lambda i, j: (0,) * len(shape))
    return pl.pallas_call(
        functools.partial(_conv_prompt_kernel, tc=tc, rc=rc),
        grid=(b, t // tc),
        in_specs=[tok, tok, const((CONV_WIDTH, D_CONV)), const((1, D_CONV)), const((1, D_CONV)),
                  const((1, D_CONV))],
        out_specs=tok,
        out_shape=jax.ShapeDtypeStruct((b, t, D_CONV), F32),
        scratch_shapes=[pltpu.VMEM((tc + CONV_HALO, D_CONV), F32)],
        compiler_params=_cparams(("arbitrary", "arbitrary")),
        name="conv_prompt",
    )(u, sgc, w_dw, b_dw, ln_g, ln_b)


def _conv_sample_kernel(st_ref, u_ref, sgc_ref, w_ref, b_ref, lg_ref, lb_ref, o_ref, ns_ref, *, ts):
    def ext(i):
        return st_ref[i] if i < CONV_STATE else u_ref[:, i - CONV_STATE, :]

    for t in range(ts):
        acc = jnp.zeros((o_ref.shape[0], D_CONV), F32)
        for j in range(CONV_WIDTH):
            acc = acc + ext(t + j) * w_ref[j:j + 1, :]
        o_ref[:, t, :] = _ln_swish(acc + b_ref[...], lg_ref, lb_ref) * sgc_ref[:, t, :]
    for i in range(CONV_STATE):
        ns_ref[i] = ext(i + ts)


def _conv_sample(state_t, u, sgc, w_dw, b_dw, ln_g, ln_b, *, bs=32):
    db, ts, _ = u.shape
    seq = pl.BlockSpec((bs, ts, D_CONV), lambda i: (i, 0, 0))
    st = pl.BlockSpec((CONV_STATE, bs, D_CONV), lambda i: (0, i, 0))
    const = lambda shape: pl.BlockSpec(shape, lambda i: (0,) * len(shape))
    return pl.pallas_call(
        functools.partial(_conv_sample_kernel, ts=ts),
        grid=(db // bs,),
        in_specs=[st, seq, seq, const((CONV_WIDTH, D_CONV)), const((1, D_CONV)), const((1, D_CONV)),
                  const((1, D_CONV))],
        out_specs=(seq, st),
        out_shape=(jax.ShapeDtypeStruct((db, ts, D_CONV), F32),
                   jax.ShapeDtypeStruct((CONV_STATE, db, D_CONV), F32)),
        compiler_params=_cparams(("arbitrary",)),
        name="conv_sample",
    )(state_t, u, sgc, w_dw, b_dw, ln_g, ln_b)


def _attn_prompt_kernel(q_ref, kt_ref, v_ref, ct_ref, o_ref, m_ref, l_ref, acc_ref, *, tq):
    pair = pl.program_id(1)
    i = pl.program_id(2)
    q0 = pl.multiple_of(i * tq, tq)
    q2 = q_ref[...]
    lane = lax.broadcasted_iota(jnp.int32, (tq, LANES), 1)
    head_row = lax.broadcasted_iota(jnp.int32, (N_HEADS, tq), 0)
    row = lax.broadcasted_iota(jnp.int32, (tq, tq), 0)
    col = lax.broadcasted_iota(jnp.int32, (tq, tq), 1)
    outs = []
    for hh in range(2):
        in_head = (lane < HEAD_DIM) if hh == 0 else (lane >= HEAD_DIM)
        qh = jnp.where(in_head, q2, jnp.zeros_like(q2))

        def c_row(k0):
            blk = ct_ref[:, pl.ds(k0, tq)]
            return jnp.sum(jnp.where(head_row == 2 * pair + hh, blk, 0.0), axis=0, keepdims=True)

        c_ref_val = c_row(q0)[:, 0:1]
        m_ref[...] = jnp.full_like(m_ref, NEG_BIG)
        l_ref[...] = jnp.zeros_like(l_ref)
        acc_ref[...] = jnp.zeros_like(acc_ref)

        def block(j, masked):
            k0 = pl.multiple_of(j * tq, tq)
            ktb = kt_ref[:, pl.ds(k0, tq)]
            vb = v_ref[pl.ds(k0, tq), :]
            s = jnp.dot(qh, ktb, preferred_element_type=F32) + (c_ref_val - c_row(k0))
            if masked:
                s = jnp.where(row >= col, s, NEG_BIG)
            m_prev = m_ref[...]
            m_new = jnp.maximum(m_prev, jnp.max(s, axis=1, keepdims=True))
            alpha = jnp.exp(m_prev - m_new)
            p = jnp.exp(s - m_new)
            l_ref[...] = alpha * l_ref[...] + jnp.sum(p, axis=1, keepdims=True)
            acc_ref[...] = alpha * acc_ref[...] + jnp.dot(p.astype(BF16), vb, preferred_element_type=F32)
            m_ref[...] = m_new

        def body(j, carry):
            block(j, False)
            return carry

        lax.fori_loop(0, i, body, 0)
        block(i, True)
        outs.append(acc_ref[...] / l_ref[...])
    o_ref[...] = jnp.where(lane < HEAD_DIM, outs[0], outs[1])


def _attn_prompt(qb, ktb, vb, ct, *, tq=512):
    b, t, _ = qb.shape
    return pl.pallas_call(
        functools.partial(_attn_prompt_kernel, tq=tq),
        grid=(b, N_HEADS // 2, t // tq),
        in_specs=[pl.BlockSpec((None, tq, LANES), lambda bi, pi, qi: (bi, qi, pi)),
                  pl.BlockSpec((None, LANES, t), lambda bi, pi, qi: (bi, pi, 0)),
                  pl.BlockSpec((None, t, LANES), lambda bi, pi, qi: (bi, 0, pi)),
                  pl.BlockSpec((None, N_HEADS, t), lambda bi, pi, qi: (bi, 0, 0))],
        out_specs=pl.BlockSpec((None, tq, LANES), lambda bi, pi, qi: (bi, qi, pi)),
        out_shape=jax.ShapeDtypeStruct((b, t, D_ATT), F32),
        scratch_shapes=[pltpu.VMEM((tq, 1), F32), pltpu.VMEM((tq, 1), F32), pltpu.VMEM((tq, LANES), F32)],
        compiler_params=_cparams(("arbitrary", "arbitrary", "arbitrary")),
        name="attn_prompt",
    )(qb, ktb, vb, ct)


def _suffix_matrix():
    after = (np.arange(PAGE_SIZE)[:, None] > np.arange(PAGE_SIZE)[None, :]).astype(np.float32)
    return np.concatenate([after, np.ones_like(after)], axis=1)


def _bias_kernel(pt_ref, smat_ref, *refs, n_pages):
    x_refs = refs[:n_pages]
    o_ref = refs[n_pages]
    lhs_ref = refs[n_pages + 1]
    for p in range(n_pages):
        lhs_ref[p * N_HEADS:(p + 1) * N_HEADS, :] = x_refs[p][...]
    lhs = lhs_ref[...]
    hi = lhs.astype(BF16)
    r1 = lhs - hi.astype(F32)
    mid = r1.astype(BF16)
    lo = (r1 - mid.astype(F32)).astype(BF16)
    smat = smat_ref[...]
    out = (jnp.dot(hi, smat, preferred_element_type=F32) + jnp.dot(mid, smat, preferred_element_type=F32)
           + jnp.dot(lo, smat, preferred_element_type=F32))
    carry = jnp.zeros((N_HEADS, PAGE_SIZE), F32)
    for p in reversed(range(n_pages)):
        blk = out[p * N_HEADS:(p + 1) * N_HEADS, :]
        o_ref[p] = blk[:, :PAGE_SIZE] + carry
        carry = carry + blk[:, PAGE_SIZE:]


def _past_bias(logf_pages_t, page_table):
    db, n_pages = page_table.shape
    smat = jnp.asarray(_suffix_matrix(), dtype=BF16)
    page_spec = lambda p: pl.BlockSpec((None, N_HEADS, PAGE_SIZE), lambda bi, pt: (pt[bi, p], 0, 0))
    grid_spec = pltpu.PrefetchScalarGridSpec(
        num_scalar_prefetch=1,
        grid=(db,),
        in_specs=[pl.BlockSpec(smat.shape, lambda bi, pt: (0, 0))] + [page_spec(p) for p in range(n_pages)],
        out_specs=pl.BlockSpec((None, n_pages, N_HEADS, PAGE_SIZE), lambda bi, pt: (bi, 0, 0, 0)),
        scratch_shapes=[pltpu.VMEM((n_pages * N_HEADS, PAGE_SIZE), F32)],
    )
    return pl.pallas_call(
        functools.partial(_bias_kernel, n_pages=n_pages),
        grid_spec=grid_spec,
        out_shape=jax.ShapeDtypeStruct((db, n_pages, N_HEADS, PAGE_SIZE), F32),
        compiler_params=_cparams(("arbitrary",)),
        name="past_bias",
    )(page_table, smat, *([logf_pages_t] * n_pages))


def _attn_sample_kernel(pt_ref, q_ref, kn_ref, vn_ref, fn_ref, bias_ref, *refs, pp, ts):
    kt_refs = refs[:pp]
    vt_refs = refs[pp:2 * pp]
    o_ref = refs[2 * pp]
    qrows_ref, m_ref, l_ref, acc_ref = refs[2 * pp + 1:]
    g = pl.program_id(1)
    nrow = ts * N_HEADS
    row = lax.broadcasted_iota(jnp.int32, (nrow, D_ATT), 0)
    lane = lax.broadcasted_iota(jnp.int32, (nrow, D_ATT), 1)
    own_head = (lane // HEAD_DIM) == (row % N_HEADS)

    @pl.when(g == 0)
    def _():
        q = q_ref[...].astype(F32)
        qb = jnp.broadcast_to(q[:, None, :], (ts, N_HEADS, D_ATT)).reshape(nrow, D_ATT)
        qrows_ref[...] = jnp.where(own_head, qb, 0.0)
        m_ref[...] = jnp.full_like(m_ref, NEG_BIG)
        l_ref[...] = jnp.zeros_like(l_ref)
        acc_ref[...] = jnp.zeros_like(acc_ref)

    qrows = qrows_ref[...]
    scores = []
    for p in range(pp):
        s = jnp.dot(qrows, kt_refs[p][...], preferred_element_type=F32)
        s = s.reshape(ts, N_HEADS, PAGE_SIZE) + bias_ref[p][None, :, :]
        scores.append(s.reshape(nrow, PAGE_SIZE))
    s_all = jnp.concatenate(scores, axis=1)
    m_prev = m_ref[...]
    m_new = jnp.maximum(m_prev, jnp.max(s_all, axis=1, keepdims=True))
    alpha = jnp.exp(m_prev - m_new)
    prob = jnp.exp(s_all - m_new)
    l_ref[...] = alpha * l_ref[...] + jnp.sum(prob, axis=1, keepdims=True)
    pv = jnp.zeros((nrow, D_ATT), F32)
    for p in range(pp):
        pv = pv + lax.dot_general(prob[:, p * PAGE_SIZE:(p + 1) * PAGE_SIZE], vt_refs[p][...], NT_DIMS,
                                  preferred_element_type=F32)
    acc_ref[...] = alpha * acc_ref[...] + pv
    m_ref[...] = m_new

    @pl.when(g == pl.num_programs(1) - 1)
    def _():
        kn = kn_ref[...].astype(BF16).astype(F32)
        vn = vn_ref[...].astype(BF16).astype(F32)
        fn = fn_ref[...]
        eye = (lax.broadcasted_iota(jnp.int32, (N_HEADS, N_HEADS), 0)
               == lax.broadcasted_iota(jnp.int32, (N_HEADS, N_HEADS), 1))
        tok = lax.broadcasted_iota(jnp.int32, (nrow, 1), 0) // N_HEADS
        logits = []
        cn = jnp.zeros((1, N_HEADS), F32)
        for j in range(ts):
            cn = cn + fn[j:j + 1, :]
            cn_col = jnp.sum(jnp.where(eye, jnp.broadcast_to(cn, (N_HEADS, N_HEADS)), 0.0), axis=1, keepdims=True)
            cn_rows = jnp.broadcast_to(cn_col[None, :, :], (ts, N_HEADS, 1)).reshape(nrow, 1)
            sj = jnp.sum(qrows * kn[j:j + 1, :], axis=1, keepdims=True) - cn_rows
            logits.append(jnp.where(tok >= j, sj, NEG_BIG))
        m_prev2 = m_ref[...]
        m_fin = m_prev2
        for sj in logits:
            m_fin = jnp.maximum(m_fin, sj)
        alpha2 = jnp.exp(m_prev2 - m_fin)
        l_fin = alpha2 * l_ref[...]
        acc = alpha2 * acc_ref[...]
        for j, sj in enumerate(logits):
            pj = jnp.exp(sj - m_fin)
            l_fin = l_fin + pj
            acc = acc + pj * vn[j:j + 1, :]
        out = jnp.where(own_head, acc / l_fin, 0.0)
        o_ref[...] = jnp.sum(out.reshape(ts, N_HEADS, D_ATT), axis=1)


def _attn_sample(q, kn, vn, fn, bias, k_pages_t, v_pages_t, page_table, *, pp=8):
    db, ts, _ = q.shape
    n_pages = page_table.shape[1]
    nrow = ts * N_HEADS
    seq = lambda r, d: pl.BlockSpec((None, r, d), lambda bi, gi, pt: (bi, 0, 0))
    page_spec = lambda p: pl.BlockSpec((None, D_ATT, PAGE_SIZE), lambda bi, gi, pt: (pt[bi, gi * pp + p], 0, 0))
    grid_spec = pltpu.PrefetchScalarGridSpec(
        num_scalar_prefetch=1,
        grid=(db, n_pages // pp),
        in_specs=[seq(ts, D_ATT), seq(ts, D_ATT), seq(ts, D_ATT), seq(ts, N_HEADS),
                  pl.BlockSpec((None, pp, N_HEADS, PAGE_SIZE), lambda bi, gi, pt: (bi, gi, 0, 0))]
                 + [page_spec(p) for p in range(pp)] * 2,
        out_specs=seq(ts, D_ATT),
        scratch_shapes=[pltpu.VMEM((nrow, D_ATT), F32), pltpu.VMEM((nrow, 1), F32), pltpu.VMEM((nrow, 1), F32),
                        pltpu.VMEM((nrow, D_ATT), F32)],
    )
    return pl.pallas_call(
        functools.partial(_attn_sample_kernel, pp=pp, ts=ts),
        grid_spec=grid_spec,
        out_shape=jax.ShapeDtypeStruct((db, ts, D_ATT), F32),
        compiler_params=_cparams(("arbitrary", "arbitrary")),
        name="attn_sample",
    )(page_table, q, kn, vn, fn, bias, *([k_pages_t] * pp), *([v_pages_t] * pp))


def _out_kernel(att_ref, sga_ref, mixc_ref, x_ref, p_ref, wo_ref, wpe_ref, gpe_ref, wpg_ref, gf_ref, y_ref):
    mix_a = (att_ref[...] * sga_ref[...]).astype(BF16)
    mix_c = mixc_ref[...].astype(BF16)
    h = (x_ref[...] + jnp.dot(mix_a, wo_ref[0:D_ATT, :], preferred_element_type=F32)
         + jnp.dot(mix_c, wo_ref[D_ATT:, :], preferred_element_type=F32))
    pe = jnp.dot(p_ref[...].astype(BF16), wpe_ref[...], preferred_element_type=F32)
    e = pe * lax.rsqrt(jnp.mean(pe * pe, axis=-1, keepdims=True) + EPS) * gpe_ref[...]
    gate = jax.nn.sigmoid(jnp.dot(h.astype(BF16), wpg_ref[...], preferred_element_type=F32))
    h = h + gate * e
    y_ref[...] = h * lax.rsqrt(jnp.mean(h * h, axis=-1, keepdims=True) + EPS) * gf_ref[...]


def _out_stage(att, sga, mixc, x, p, w_out, w_pe, g_pe, w_pg, g_final, *, tm):
    n = x.shape[0]
    tok = lambda d: pl.BlockSpec((tm, d), lambda i: (i, 0))
    const = lambda shape: pl.BlockSpec(shape, lambda i: (0,) * len(shape))
    return pl.pallas_call(
        _out_kernel,
        grid=(n // tm,),
        in_specs=[tok(D_ATT), tok(D_ATT), tok(D_CONV), tok(D_MODEL), tok(D_PLE),
                  const((D_ATT + D_CONV, D_MODEL)), const((D_PLE, D_MODEL)), const((1, D_MODEL)),
                  const((D_MODEL, D_MODEL)), const((1, D_MODEL))],
        out_specs=tok(D_MODEL),
        out_shape=jax.ShapeDtypeStruct((n, D_MODEL), F32),
        compiler_params=_cparams(("arbitrary",)),
        name="out_stage",
    )(att, sga, mixc, x, p, w_out, w_pe, g_pe, w_pg, g_final)


def _in_weights(w_in, b_f):
    a, c, h = D_ATT, D_CONV, N_HEADS
    w = w_in.astype(BF16)
    q, k, v = w[:, 0:a], w[:, a:2 * a], w[:, 2 * a:3 * a]
    f = w[:, 3 * a:3 * a + h]
    o = 3 * a + h
    rest = w[:, o:]
    w_prompt = jnp.concatenate([q, v, rest], axis=1)
    wkt = k.T
    wft = jnp.pad(f.T, ((0, BF16_SUBLANES - h), (0, 0)))
    bf_col = jnp.pad(b_f, (0, BF16_SUBLANES - h)).reshape(BF16_SUBLANES, 1)
    w_sample = jnp.concatenate([q, k, v, rest, jnp.pad(f, ((0, 0), (0, F_PAD - h)))], axis=1)
    bf_row = jnp.pad(b_f, (0, F_PAD - h)).reshape(1, F_PAD)
    return w_prompt, wkt, wft, bf_col, w_sample, bf_row


def kernel(x_prompt, x_sample, cache_k, cache_v, cache_logf, state_conv, page_table, p_prompt, p_sample,
           g_norm, w_in, b_f, w_dw, b_dw, ln_g, ln_b, w_out, w_pe, g_pe, w_pg, g_final):
    assert w_in.shape[0] == 1, "single-layer step"
    b, t, _ = x_prompt.shape
    db, ts, _ = x_sample.shape
    n_phys = cache_k.shape[1]
    row = lambda a: a.reshape(1, -1)

    w_prompt, wkt, wft, bf_col, w_sample, bf_row = _in_weights(w_in[0], b_f[0])
    wo, wpe, wpg = w_out[0].astype(BF16), w_pe[0].astype(BF16), w_pg[0].astype(BF16)
    gn, gpe, gf = row(g_norm[0]), row(g_pe[0]), row(g_final)
    bdw, lng, lnb = row(b_dw[0]), row(ln_g[0]), row(ln_b[0])

    qb, vb, v_p, ktb, kt_p, logft_p, ct, sga_p, u_p, sgc_p = _proj_prompt(x_prompt, gn, w_prompt, wkt, wft, bf_col)
    mixc_p = _conv_prompt(u_p, sgc_p, w_dw[0], bdw, lng, lnb)
    att_p = _attn_prompt(qb, ktb, vb, ct)
    n_p = b * t
    y_p = _out_stage(att_p.reshape(n_p, D_ATT), sga_p.reshape(n_p, D_ATT), mixc_p.reshape(n_p, D_CONV),
                     x_prompt.reshape(n_p, D_MODEL), p_prompt[0].reshape(n_p, D_PLE),
                     wo, wpe, gpe, wpg, gf, tm=512).reshape(b, t, D_MODEL)
    k_prompt = jnp.transpose(kt_p.reshape(b, N_HEADS, HEAD_DIM, t), (0, 3, 1, 2))[None]
    v_prompt = v_p.reshape(1, b, t, N_HEADS, HEAD_DIM)
    logf_prompt = jnp.transpose(logft_p, (0, 2, 1))[None]
    conv_prompt = u_p[:, t - CONV_STATE:, :][None]

    n_s = db * ts
    qb_s, k_s, v_s, logf_s, sga_s, u_s, sgc_s = _proj_sample(x_sample.reshape(n_s, D_MODEL), gn, w_sample, bf_row)
    seqs = lambda a: a.reshape(db, ts, a.shape[-1])
    state_t = jnp.transpose(state_conv[0], (1, 0, 2))
    mixc_s, new_state_t = _conv_sample(state_t, seqs(u_s), seqs(sgc_s), w_dw[0], bdw, lng, lnb)
    k_pages_t = jnp.transpose(cache_k[0], (0, 2, 3, 1)).reshape(n_phys, D_ATT, PAGE_SIZE)
    v_pages_t = jnp.transpose(cache_v[0], (0, 2, 3, 1)).reshape(n_phys, D_ATT, PAGE_SIZE)
    logf_pages_t = jnp.transpose(cache_logf[0], (0, 2, 1))
    bias = _past_bias(logf_pages_t, page_table)
    att_s = _attn_sample(seqs(qb_s), seqs(k_s), seqs(v_s), seqs(logf_s), bias, k_pages_t, v_pages_t, page_table)
    y_s = _out_stage(att_s.reshape(n_s, D_ATT), sga_s, mixc_s.reshape(n_s, D_CONV),
                     x_sample.reshape(n_s, D_MODEL), p_sample[0].reshape(n_s, D_PLE),
                     wo, wpe, gpe, wpg, gf, tm=n_s).reshape(db, ts, D_MODEL)
    conv_sample = jnp.transpose(new_state_t, (1, 0, 2))[None]

    heads = lambda a: a.reshape(1, db, ts, N_HEADS, HEAD_DIM)
    return (y_p, y_s, k_prompt, v_prompt, logf_prompt, conv_prompt,
            heads(k_s), heads(v_s), logf_s.reshape(1, db, ts, N_HEADS), conv_sample)
```

```python
import functools

import numpy as np
import jax
import jax.numpy as jnp
from jax import lax
from jax.experimental import pallas as pl
from jax.experimental.pallas import tpu as pltpu

D_MODEL = 1024
D_ATT = 512
D_CONV = 512
N_HEADS = 8
HEAD_DIM = 64
CONV_WIDTH = 31
CONV_STATE = CONV_WIDTH - 1
D_PLE = 256
PAGE_SIZE = 128
EPS = 1e-6
ATT_SCALE = HEAD_DIM ** -0.5
LOG2E = 1.4426950408889634
Q_PRESCALE = ATT_SCALE * LOG2E
NEG_BIG = -1e30

LANES = 128
SUBLANES = 8
BF16_SUBLANES = 16
F_PAD = LANES

VMEM_LIMIT = 56 * 1024 * 1024

F32 = jnp.float32
BF16 = jnp.bfloat16
NT_DIMS = (((1,), (1,)), ((), ()))


def _silu(x):
    return x * jax.nn.sigmoid(x)


def _cparams(sem):
    return pltpu.CompilerParams(dimension_semantics=sem, vmem_limit_bytes=VMEM_LIMIT)


def _rms_normed(x_ref, g_ref):
    x = x_ref[...]
    ms = jnp.mean(x * x, axis=-1, keepdims=True)
    return ((x * lax.rsqrt(ms + EPS)) * g_ref[...]).astype(BF16)


PW_Q, PW_V, PW_GA, PW_GLUA, PW_GLUB, PW_GC = 0, 512, 1024, 1536, 2048, 2560
PW_COLS = 3072


def _proj_prompt_kernel(x_ref, g_ref, w_ref, wkt_ref, wft_ref, bf_ref,
                        qb_ref, vb_ref, vf_ref, ktb_ref, ktf_ref, logft_ref, ct_ref, sga_ref, u_ref, sgc_ref,
                        carry_ref, *, tm):
    xn = _rms_normed(x_ref, g_ref)

    def proj(lo, width):
        return jnp.dot(xn, w_ref[:, lo:lo + width], preferred_element_type=F32)

    qb_ref[...] = (proj(PW_Q, D_ATT) * Q_PRESCALE).astype(BF16)
    v = proj(PW_V, D_ATT)
    vf_ref[...] = v
    vb_ref[...] = v.astype(BF16)
    sga_ref[...] = _silu(proj(PW_GA, D_ATT))
    u_ref[...] = proj(PW_GLUA, D_CONV) * jax.nn.sigmoid(proj(PW_GLUB, D_CONV))
    sgc_ref[...] = _silu(proj(PW_GC, D_CONV))

    kt = lax.dot_general(wkt_ref[...], xn, NT_DIMS, preferred_element_type=F32)
    ktf_ref[...] = kt
    ktb_ref[...] = kt.astype(BF16)
    fzt = lax.dot_general(wft_ref[...], xn, NT_DIMS, preferred_element_type=F32)
    logft = jax.nn.log_sigmoid(fzt + bf_ref[...])[:N_HEADS, :]
    logft_ref[...] = logft

    @pl.when(pl.program_id(1) == 0)
    def _():
        carry_ref[...] = jnp.zeros_like(carry_ref)

    lane = lax.broadcasted_iota(jnp.int32, (N_HEADS, tm), 1)
    c = logft
    shift = 1
    while shift < tm:
        c = c + jnp.where(lane >= shift, pltpu.roll(c, shift, axis=1), 0.0)
        shift *= 2
    c = c + carry_ref[:, 0:1]
    ct_ref[...] = c
    carry_ref[...] = jnp.broadcast_to(c[:, tm - 1:tm], carry_ref.shape)


def _proj_prompt(x, g_norm, w_tok, wkt, wft, bf_col, *, tm=512):
    b, t, _ = x.shape
    tok = lambda d: pl.BlockSpec((None, tm, d), lambda i, j: (i, j, 0))
    tr = lambda r: pl.BlockSpec((None, r, tm), lambda i, j: (i, 0, j))
    const = lambda shape: pl.BlockSpec(shape, lambda i, j: (0,) * len(shape))
    out_shape = (
        jax.ShapeDtypeStruct((b, t, D_ATT), BF16),
        jax.ShapeDtypeStruct((b, t, D_ATT), BF16),
        jax.ShapeDtypeStruct((b, t, D_ATT), F32),
        jax.ShapeDtypeStruct((b, D_ATT, t), BF16),
        jax.ShapeDtypeStruct((b, D_ATT, t), F32),
        jax.ShapeDtypeStruct((b, N_HEADS, t), F32),
        jax.ShapeDtypeStruct((b, N_HEADS, t), F32),
        jax.ShapeDtypeStruct((b, t, D_ATT), F32),
        jax.ShapeDtypeStruct((b, t, D_CONV), F32),
        jax.ShapeDtypeStruct((b, t, D_CONV), F32),
    )
    out_specs = (tok(D_ATT), tok(D_ATT), tok(D_ATT), tr(D_ATT), tr(D_ATT), tr(N_HEADS), tr(N_HEADS),
                 tok(D_ATT), tok(D_CONV), tok(D_CONV))
    return pl.pallas_call(
        functools.partial(_proj_prompt_kernel, tm=tm),
        grid=(b, t // tm),
        in_specs=[tok(D_MODEL), const((1, D_MODEL)), const(w_tok.shape), const(wkt.shape), const(wft.shape),
                  const(bf_col.shape)],
        out_specs=out_specs,
        out_shape=out_shape,
        scratch_shapes=[pltpu.VMEM((N_HEADS, LANES), F32)],
        compiler_params=_cparams(("arbitrary", "arbitrary")),
        name="proj_prompt",
    )(x, g_norm, w_tok, wkt, wft, bf_col)


SW_Q, SW_K, SW_V, SW_GA, SW_GLUA, SW_GLUB, SW_GC, SW_F = 0, 512, 1024, 1536, 2048, 2560, 3072, 3584
SW_COLS = SW_F + F_PAD


def _proj_sample_kernel(x_ref, g_ref, w_ref, bf_ref, qb_ref, k_ref, v_ref, logf_ref, sga_ref, u_ref, sgc_ref):
    xn = _rms_normed(x_ref, g_ref)

    def proj(lo, width):
        return jnp.dot(xn, w_ref[:, lo:lo + width], preferred_element_type=F32)

    qb_ref[...] = (proj(SW_Q, D_ATT) * ATT_SCALE).astype(BF16)
    k_ref[...] = proj(SW_K, D_ATT)
    v_ref[...] = proj(SW_V, D_ATT)
    sga_ref[...] = _silu(proj(SW_GA, D_ATT))
    u_ref[...] = proj(SW_GLUA, D_CONV) * jax.nn.sigmoid(proj(SW_GLUB, D_CONV))
    sgc_ref[...] = _silu(proj(SW_GC, D_CONV))
    logf_ref[...] = jax.nn.log_sigmoid(proj(SW_F, F_PAD) + bf_ref[...])[:, :N_HEADS]


def _proj_sample(x, g_norm, w_tok, bf_row):
    n = x.shape[0]
    full = lambda shape: pl.BlockSpec(shape, lambda i: (0,) * len(shape))
    wide = jax.ShapeDtypeStruct((n, D_ATT), F32)
    out_shape = (jax.ShapeDtypeStruct((n, D_ATT), BF16), wide, wide, jax.ShapeDtypeStruct((n, N_HEADS), F32),
                 wide, wide, wide)
    return pl.pallas_call(
        _proj_sample_kernel,
        grid=(1,),
        in_specs=[full(x.shape), full(g_norm.shape), full(w_tok.shape), full(bf_row.shape)],
        out_specs=tuple(full(s.shape) for s in out_shape),
        out_shape=out_shape,
        compiler_params=_cparams(("arbitrary",)),
        name="proj_sample",
    )(x, g_norm, w_tok, bf_row)


CONV_HALO = 32


def _ln_swish(y, lg_ref, lb_ref):
    mu = jnp.mean(y, axis=-1, keepdims=True)
    yc = y - mu
    var = jnp.mean(yc * yc, axis=-1, keepdims=True)
    return _silu(yc * lax.rsqrt(var + EPS) * lg_ref[...] + lb_ref[...])


def _conv_prompt_kernel(u_ref, sgc_ref, w_ref, b_ref, lg_ref, lb_ref, o_ref, ext_ref, shift_ref, *, tc, rc):
    @pl.when(pl.program_id(1) == 0)
    def _():
        ext_ref[0:CONV_HALO, :] = jnp.zeros((CONV_HALO, D_CONV), F32)

    ext_ref[CONV_HALO:CONV_HALO + tc, :] = u_ref[...]
    off = CONV_HALO - CONV_STATE

    def chunk(ci, slot):
        r0 = pl.multiple_of(ci * rc, rc)
        win = ext_ref[pl.ds(r0, rc + CONV_HALO), :]
        span = rc + CONV_HALO - SUBLANES
        for r in range(1, SUBLANES):
            shift_ref[slot, r - 1] = win[r:r + span, :]
        acc = jnp.zeros((rc, D_CONV), F32)
        for j in range(CONV_WIDTH):
            r = (off + j) % SUBLANES
            a = off + j - r
            tap = win[a:a + rc, :] if r == 0 else shift_ref[slot, r - 1, a:a + rc, :]
            acc = acc + tap * w_ref[j:j + 1, :]
        o_ref[pl.ds(r0, rc), :] = _ln_swish(acc + b_ref[...], lg_ref, lb_ref) * sgc_ref[pl.ds(r0, rc), :]

    def chunk_pair(cp, carry):
        chunk(2 * cp, 0)
        chunk(2 * cp + 1, 1)
        return carry

    lax.fori_loop(0, tc // (2 * rc), chunk_pair, 0)
    ext_ref[0:CONV_HALO, :] = ext_ref[tc:tc + CONV_HALO, :]


def _conv_prompt(u, sgc, w_dw, b_dw, ln_g, ln_b, *, tc=512, rc=32):
    b, t, _ = u.shape
    tok = pl.BlockSpec((None, tc, D_CONV), lambda i, j: (i, j, 0))
    const = lambda shape: pl.BlockSpec(shape, lambda i, j: (0,) * len(shape))
    return pl.pallas_call(
        functools.partial(_conv_prompt_kernel, tc=tc, rc=rc),
        grid=(b, t // tc),
        in_specs=[tok, tok, const((CONV_WIDTH, D_CONV)), const((1, D_CONV)), const((1, D_CONV)),
                  const((1, D_CONV))],
        out_specs=tok,
        out_shape=jax.ShapeDtypeStruct((b, t, D_CONV), F32),
        scratch_shapes=[pltpu.VMEM((tc + CONV_HALO, D_CONV), F32),
                        pltpu.VMEM((2, SUBLANES - 1, rc + CONV_HALO - SUBLANES, D_CONV), F32)],
        compiler_params=_cparams(("arbitrary", "arbitrary")),
        name="conv_prompt",
    )(u, sgc, w_dw, b_dw, ln_g, ln_b)


def _conv_sample_kernel(st_ref, u_ref, sgc_ref, w_ref, b_ref, lg_ref, lb_ref, o_ref, ns_ref, *, ts):
    def ext(i):
        return st_ref[i] if i < CONV_STATE else u_ref[:, i - CONV_STATE, :]

    for t in range(ts):
        acc = jnp.zeros((o_ref.shape[0], D_CONV), F32)
        for j in range(CONV_WIDTH):
            acc = acc + ext(t + j) * w_ref[j:j + 1, :]
        o_ref[:, t, :] = _ln_swish(acc + b_ref[...], lg_ref, lb_ref) * sgc_ref[:, t, :]
    for i in range(CONV_STATE):
        ns_ref[i] = ext(i + ts)


def _conv_sample(state_t, u, sgc, w_dw, b_dw, ln_g, ln_b, *, bs=32):
    db, ts, _ = u.shape
    seq = pl.BlockSpec((bs, ts, D_CONV), lambda i: (i, 0, 0))
    st = pl.BlockSpec((CONV_STATE, bs, D_CONV), lambda i: (0, i, 0))
    const = lambda shape: pl.BlockSpec(shape, lambda i: (0,) * len(shape))
    return pl.pallas_call(
        functools.partial(_conv_sample_kernel, ts=ts),
        grid=(db // bs,),
        in_specs=[st, seq, seq, const((CONV_WIDTH, D_CONV)), const((1, D_CONV)), const((1, D_CONV)),
                  const((1, D_CONV))],
        out_specs=(seq, st),
        out_shape=(jax.ShapeDtypeStruct((db, ts, D_CONV), F32),
                   jax.ShapeDtypeStruct((CONV_STATE, db, D_CONV), F32)),
        compiler_params=_cparams(("arbitrary",)),
        name="conv_sample",
    )(state_t, u, sgc, w_dw, b_dw, ln_g, ln_b)


def _attn_prompt_kernel(q_ref, kt_ref, v_ref, ct_ref, o_ref, m_ref, acc_ref, *, tq, rq):
    pair = pl.program_id(1)
    i = pl.program_id(2)
    q0 = pl.multiple_of(i * tq, tq)
    q2 = q_ref[...]
    lane = lax.broadcasted_iota(jnp.int32, (tq, LANES), 1)
    in_head = (lane < HEAD_DIM, lane >= HEAD_DIM)
    qh = [jnp.where(in_head[hh], q2, jnp.zeros_like(q2)) for hh in range(2)]
    head_row = lax.broadcasted_iota(jnp.int32, (N_HEADS, tq), 0)

    def c_rows(k0):
        blk = ct_ref[:, pl.ds(k0, tq)] * LOG2E
        return [jnp.sum(jnp.where(head_row == 2 * pair + hh, blk, 0.0), axis=0, keepdims=True) for hh in range(2)]

    c_first = [r[:, 0:1] for r in c_rows(q0)]
    m_ref[...] = jnp.full_like(m_ref, NEG_BIG)
    acc_ref[...] = jnp.zeros_like(acc_ref)

    def block(j, masked):
        k0 = pl.multiple_of(j * tq, tq)
        ktb = kt_ref[:, pl.ds(k0, tq)]
        vb = v_ref[pl.ds(k0, tq), :]
        c_keys = c_rows(k0)
        for hh in range(2):
            vh = jnp.where(in_head[hh], vb, jnp.ones_like(vb))
            bias = c_first[hh] - c_keys[hh]
            for r0 in range(0, tq, rq):
                nk = min(r0 + rq, tq) if masked else tq
                s = jnp.dot(qh[hh][r0:r0 + rq, :], ktb[:, :nk], preferred_element_type=F32) + bias[:, :nk]
                if masked:
                    row = lax.broadcasted_iota(jnp.int32, (rq, nk), 0) + r0
                    col = lax.broadcasted_iota(jnp.int32, (rq, nk), 1)
                    s = jnp.where(row >= col, s, NEG_BIG)
                m_prev = m_ref[hh, r0:r0 + rq, :]
                m_new = jnp.maximum(m_prev, jnp.max(s, axis=1, keepdims=True))
                alpha = jnp.exp2(m_prev - m_new)
                p = jnp.concatenate([jnp.exp2(s[:, c * LANES:(c + 1) * LANES] - m_new)
                                     for c in range(nk // LANES)], axis=1)
                acc_ref[hh, r0:r0 + rq, :] = alpha * acc_ref[hh, r0:r0 + rq, :] + jnp.dot(
                    p.astype(BF16), vh[:nk, :], preferred_element_type=F32)
                m_ref[hh, r0:r0 + rq, :] = m_new

    def body(j, carry):
        block(j, False)
        return carry

    lax.fori_loop(0, i, body, 0)
    block(i, True)
    acc0, acc1 = acc_ref[0], acc_ref[1]
    o_ref[...] = jnp.where(in_head[0], acc0 / pltpu.roll(acc0, HEAD_DIM, axis=1),
                           acc1 / pltpu.roll(acc1, HEAD_DIM, axis=1))


def _attn_prompt(qb, ktb, vb, ct, *, tq=512, rq=512):
    b, t, _ = qb.shape
    return pl.pallas_call(
        functools.partial(_attn_prompt_kernel, tq=tq, rq=rq),
        grid=(b, N_HEADS // 2, t // tq),
        in_specs=[pl.BlockSpec((None, tq, LANES), lambda bi, pi, qi: (bi, qi, pi)),
                  pl.BlockSpec((None, LANES, t), lambda bi, pi, qi: (bi, pi, 0)),
                  pl.BlockSpec((None, t, LANES), lambda bi, pi, qi: (bi, 0, pi)),
                  pl.BlockSpec((None, N_HEADS, t), lambda bi, pi, qi: (bi, 0, 0))],
        out_specs=pl.BlockSpec((None, tq, LANES), lambda bi, pi, qi: (bi, qi, pi)),
        out_shape=jax.ShapeDtypeStruct((b, t, D_ATT), F32),
        scratch_shapes=[pltpu.VMEM((2, tq, LANES), F32), pltpu.VMEM((2, tq, LANES), F32)],
        compiler_params=_cparams(("arbitrary", "arbitrary", "arbitrary")),
        name="attn_prompt",
    )(qb, ktb, vb, ct)


def _suffix_matrix():
    after = (np.arange(PAGE_SIZE)[:, None] > np.arange(PAGE_SIZE)[None, :]).astype(np.float32)
    return np.concatenate([after, np.ones_like(after)], axis=1)


def _attn_sample_kernel(pt_ref, smat_ref, q_ref, kn_ref, vn_ref, fn_ref, *refs, pp, ts, ns):
    page_refs = refs[:3 * pp * ns]
    o_ref = refs[3 * pp * ns]
    qrows_ref, m_ref, l_ref, acc_ref, after_ref = refs[3 * pp * ns + 1:]
    g = pl.program_id(1)
    nrow = ts * N_HEADS
    row = lax.broadcasted_iota(jnp.int32, (nrow, D_ATT), 0)
    lane = lax.broadcasted_iota(jnp.int32, (nrow, D_ATT), 1)
    own_head = (lane // HEAD_DIM) == (row % N_HEADS)

    @pl.when(g == 0)
    def _():
        for sq in range(ns):
            q = q_ref[sq].astype(F32)
            qb = jnp.broadcast_to(q[:, None, :], (ts, N_HEADS, D_ATT)).reshape(nrow, D_ATT)
            qrows_ref[sq] = jnp.where(own_head, qb, 0.0)
        m_ref[...] = jnp.full_like(m_ref, NEG_BIG)
        l_ref[...] = jnp.zeros_like(l_ref)
        acc_ref[...] = jnp.zeros_like(acc_ref)
        after_ref[...] = jnp.zeros_like(after_ref)

    smat = smat_ref[...]
    for sq in range(ns):
        f_refs = page_refs[(3 * sq) * pp:(3 * sq + 1) * pp]
        kt_refs = page_refs[(3 * sq + 1) * pp:(3 * sq + 2) * pp]
        vt_refs = page_refs[(3 * sq + 2) * pp:(3 * sq + 3) * pp]
        lhs = jnp.concatenate([f_refs[p][...] for p in range(pp)], axis=0)
        hi = lhs.astype(BF16)
        r1 = lhs - hi.astype(F32)
        mid = r1.astype(BF16)
        lo = (r1 - mid.astype(F32)).astype(BF16)
        sums = (jnp.dot(hi, smat, preferred_element_type=F32) + jnp.dot(mid, smat, preferred_element_type=F32)
                + jnp.dot(lo, smat, preferred_element_type=F32))
        after = after_ref[sq]
        bias = [None] * pp
        for p in reversed(range(pp)):
            blk = sums[p * N_HEADS:(p + 1) * N_HEADS, :]
            bias[p] = blk[:, :PAGE_SIZE] + after
            after = after + blk[:, PAGE_SIZE:]
        after_ref[sq] = after

        qrows = qrows_ref[sq]
        scores = []
        for p in range(pp):
            s = jnp.dot(qrows, kt_refs[p][...], preferred_element_type=F32)
            s = s.reshape(ts, N_HEADS, PAGE_SIZE) + bias[p][None, :, :]
            scores.append(s.reshape(nrow, PAGE_SIZE))
        s_all = jnp.concatenate(scores, axis=1)
        m_prev = m_ref[sq]
        m_new = jnp.maximum(m_prev, jnp.max(s_all, axis=1, keepdims=True))
        alpha = jnp.exp(m_prev - m_new)
        prob = jnp.exp(s_all - m_new)
        l_ref[sq] = alpha * l_ref[sq] + jnp.sum(prob, axis=1, keepdims=True)
        pv = jnp.zeros((nrow, D_ATT), F32)
        for p in range(pp):
            pv = pv + lax.dot_general(prob[:, p * PAGE_SIZE:(p + 1) * PAGE_SIZE], vt_refs[p][...], NT_DIMS,
                                      preferred_element_type=F32)
        acc_ref[sq] = alpha * acc_ref[sq] + pv
        m_ref[sq] = m_new

    @pl.when(g == pl.num_programs(1) - 1)
    def _():
        for sq in range(ns):
            _attn_sample_new_keys(sq, kn_ref, vn_ref, fn_ref, o_ref, qrows_ref, m_ref, l_ref, acc_ref, own_head, ts)


def _attn_sample_new_keys(sq, kn_ref, vn_ref, fn_ref, o_ref, qrows_ref, m_ref, l_ref, acc_ref, own_head, ts):
    nrow = ts * N_HEADS
    qrows = qrows_ref[sq]
    kn = kn_ref[sq].astype(BF16).astype(F32)
    vn = vn_ref[sq].astype(BF16).astype(F32)
    fn = fn_ref[sq]
    eye = (lax.broadcasted_iota(jnp.int32, (N_HEADS, N_HEADS), 0)
           == lax.broadcasted_iota(jnp.int32, (N_HEADS, N_HEADS), 1))
    tok = lax.broadcasted_iota(jnp.int32, (nrow, 1), 0) // N_HEADS
    logits = []
    cn = jnp.zeros((1, N_HEADS), F32)
    for j in range(ts):
        cn = cn + fn[j:j + 1, :]
        cn_col = jnp.sum(jnp.where(eye, jnp.broadcast_to(cn, (N_HEADS, N_HEADS)), 0.0), axis=1, keepdims=True)
        cn_rows = jnp.broadcast_to(cn_col[None, :, :], (ts, N_HEADS, 1)).reshape(nrow, 1)
        sj = jnp.sum(qrows * kn[j:j + 1, :], axis=1, keepdims=True) - cn_rows
        logits.append(jnp.where(tok >= j, sj, NEG_BIG))
    m_prev = m_ref[sq]
    m_fin = m_prev
    for sj in logits:
        m_fin = jnp.maximum(m_fin, sj)
    alpha = jnp.exp(m_prev - m_fin)
    l_fin = alpha * l_ref[sq]
    acc = alpha * acc_ref[sq]
    for j, sj in enumerate(logits):
        pj = jnp.exp(sj - m_fin)
        l_fin = l_fin + pj
        acc = acc + pj * vn[j:j + 1, :]
    out = jnp.where(own_head, acc / l_fin, 0.0)
    o_ref[sq] = jnp.sum(out.reshape(ts, N_HEADS, D_ATT), axis=1)


def _attn_sample(q, kn, vn, fn, logf_pages_t, k_pages_t, v_pages_t, page_table, *, pp=8, ns=2):
    db, ts, _ = q.shape
    n_pages = page_table.shape[1]
    n_groups = n_pages // pp
    nrow = ts * N_HEADS
    smat = jnp.asarray(_suffix_matrix(), dtype=BF16)
    seq = lambda r, d: pl.BlockSpec((ns, r, d), lambda bi, gi, pt: (bi, 0, 0))

    def page_spec(rows, sq, p):
        return pl.BlockSpec((None, rows, PAGE_SIZE),
                            lambda bi, gi, pt: (pt[bi * ns + sq, (n_groups - 1 - gi) * pp + p], 0, 0))

    page_specs, page_args = [], []
    for sq in range(ns):
        for rows, arr in ((N_HEADS, logf_pages_t), (D_ATT, k_pages_t), (D_ATT, v_pages_t)):
            page_specs += [page_spec(rows, sq, p) for p in range(pp)]
            page_args += [arr] * pp
    grid_spec = pltpu.PrefetchScalarGridSpec(
        num_scalar_prefetch=1,
        grid=(db // ns, n_groups),
        in_specs=[pl.BlockSpec(smat.shape, lambda bi, gi, pt: (0, 0)),
                  seq(ts, D_ATT), seq(ts, D_ATT), seq(ts, D_ATT), seq(ts, N_HEADS)] + page_specs,
        out_specs=seq(ts, D_ATT),
        scratch_shapes=[pltpu.VMEM((ns, nrow, D_ATT), F32), pltpu.VMEM((ns, nrow, 1), F32),
                        pltpu.VMEM((ns, nrow, 1), F32), pltpu.VMEM((ns, nrow, D_ATT), F32),
                        pltpu.VMEM((ns, N_HEADS, PAGE_SIZE), F32)],
    )
    return pl.pallas_call(
        functools.partial(_attn_sample_kernel, pp=pp, ts=ts, ns=ns),
        grid_spec=grid_spec,
        out_shape=jax.ShapeDtypeStruct((db, ts, D_ATT), F32),
        compiler_params=_cparams(("arbitrary", "arbitrary")),
        name="attn_sample",
    )(page_table, smat, q, kn, vn, fn, *page_args)


def _out_kernel(att_ref, sga_ref, mixc_ref, x_ref, p_ref, wo_ref, wpe_ref, gpe_ref, wpg_ref, gf_ref, y_ref):
    mix_a = (att_ref[...] * sga_ref[...]).astype(BF16)
    mix_c = mixc_ref[...].astype(BF16)
    h = (x_ref[...] + jnp.dot(mix_a, wo_ref[0:D_ATT, :], preferred_element_type=F32)
         + jnp.dot(mix_c, wo_ref[D_ATT:, :], preferred_element_type=F32))
    pe = jnp.dot(p_ref[...].astype(BF16), wpe_ref[...], preferred_element_type=F32)
    e = pe * lax.rsqrt(jnp.mean(pe * pe, axis=-1, keepdims=True) + EPS) * gpe_ref[...]
    gate = jax.nn.sigmoid(jnp.dot(h.astype(BF16), wpg_ref[...], preferred_element_type=F32))
    h = h + gate * e
    y_ref[...] = h * lax.rsqrt(jnp.mean(h * h, axis=-1, keepdims=True) + EPS) * gf_ref[...]


def _out_stage(att, sga, mixc, x, p, w_out, w_pe, g_pe, w_pg, g_final, *, tm):
    n = x.shape[0]
    tok = lambda d: pl.BlockSpec((tm, d), lambda i: (i, 0))
    const = lambda shape: pl.BlockSpec(shape, lambda i: (0,) * len(shape))
    return pl.pallas_call(
        _out_kernel,
        grid=(n // tm,),
        in_specs=[tok(D_ATT), tok(D_ATT), tok(D_CONV), tok(D_MODEL), tok(D_PLE),
                  const((D_ATT + D_CONV, D_MODEL)), const((D_PLE, D_MODEL)), const((1, D_MODEL)),
                  const((D_MODEL, D_MODEL)), const((1, D_MODEL))],
        out_specs=tok(D_MODEL),
        out_shape=jax.ShapeDtypeStruct((n, D_MODEL), F32),
        compiler_params=_cparams(("arbitrary",)),
        name="out_stage",
    )(att, sga, mixc, x, p, w_out, w_pe, g_pe, w_pg, g_final)


def _in_weights(w_in, b_f):
    a, c, h = D_ATT, D_CONV, N_HEADS
    w = w_in.astype(BF16)
    q, k, v = w[:, 0:a], w[:, a:2 * a], w[:, 2 * a:3 * a]
    f = w[:, 3 * a:3 * a + h]
    o = 3 * a + h
    rest = w[:, o:]
    w_prompt = jnp.concatenate([q, v, rest], axis=1)
    wkt = k.T
    wft = jnp.pad(f.T, ((0, BF16_SUBLANES - h), (0, 0)))
    bf_col = jnp.pad(b_f, (0, BF16_SUBLANES - h)).reshape(BF16_SUBLANES, 1)
    w_sample = jnp.concatenate([q, k, v, rest, jnp.pad(f, ((0, 0), (0, F_PAD - h)))], axis=1)
    bf_row = jnp.pad(b_f, (0, F_PAD - h)).reshape(1, F_PAD)
    return w_prompt, wkt, wft, bf_col, w_sample, bf_row


def kernel(x_prompt, x_sample, cache_k, cache_v, cache_logf, state_conv, page_table, p_prompt, p_sample,
           g_norm, w_in, b_f, w_dw, b_dw, ln_g, ln_b, w_out, w_pe, g_pe, w_pg, g_final):
    assert w_in.shape[0] == 1, "single-layer step"
    b, t, _ = x_prompt.shape
    db, ts, _ = x_sample.shape
    n_phys = cache_k.shape[1]
    row = lambda a: a.reshape(1, -1)

    w_prompt, wkt, wft, bf_col, w_sample, bf_row = _in_weights(w_in[0], b_f[0])
    wo, wpe, wpg = w_out[0].astype(BF16), w_pe[0].astype(BF16), w_pg[0].astype(BF16)
    gn, gpe, gf = row(g_norm[0]), row(g_pe[0]), row(g_final)
    bdw, lng, lnb = row(b_dw[0]), row(ln_g[0]), row(ln_b[0])

    qb, vb, v_p, ktb, kt_p, logft_p, ct, sga_p, u_p, sgc_p = _proj_prompt(x_prompt, gn, w_prompt, wkt, wft, bf_col)
    mixc_p = _conv_prompt(u_p, sgc_p, w_dw[0], bdw, lng, lnb)
    att_p = _attn_prompt(qb, ktb, vb, ct)
    n_p = b * t
    y_p = _out_stage(att_p.reshape(n_p, D_ATT), sga_p.reshape(n_p, D_ATT), mixc_p.reshape(n_p, D_CONV),
                     x_prompt.reshape(n_p, D_MODEL), p_prompt[0].reshape(n_p, D_PLE),
                     wo, wpe, gpe, wpg, gf, tm=512).reshape(b, t, D_MODEL)
    k_prompt = jnp.transpose(kt_p.reshape(b, N_HEADS, HEAD_DIM, t), (0, 3, 1, 2))[None]
    v_prompt = v_p.reshape(1, b, t, N_HEADS, HEAD_DIM)
    logf_prompt = jnp.transpose(logft_p, (0, 2, 1))[None]
    conv_prompt = u_p[:, t - CONV_STATE:, :][None]

    n_s = db * ts
    qb_s, k_s, v_s, logf_s, sga_s, u_s, sgc_s = _proj_sample(x_sample.reshape(n_s, D_MODEL), gn, w_sample, bf_row)
    seqs = lambda a: a.reshape(db, ts, a.shape[-1])
    state_t = jnp.transpose(state_conv[0], (1, 0, 2))
    mixc_s, new_state_t = _conv_sample(state_t, seqs(u_s), seqs(sgc_s), w_dw[0], bdw, lng, lnb)
    k_pages_t = jnp.transpose(cache_k[0], (0, 2, 3, 1)).reshape(n_phys, D_ATT, PAGE_SIZE)
    v_pages_t = jnp.transpose(cache_v[0], (0, 2, 3, 1)).reshape(n_phys, D_ATT, PAGE_SIZE)
    logf_pages_t = jnp.transpose(cache_logf[0], (0, 2, 1))
    att_s = _attn_sample(seqs(qb_s), seqs(k_s), seqs(v_s), seqs(logf_s), logf_pages_t, k_pages_t, v_pages_t,
                         page_table)
    y_s = _out_stage(att_s.reshape(n_s, D_ATT), sga_s, mixc_s.reshape(n_s, D_CONV),
                     x_sample.reshape(n_s, D_MODEL), p_sample[0].reshape(n_s, D_PLE),
                     wo, wpe, gpe, wpg, gf, tm=n_s).reshape(db, ts, D_MODEL)
    conv_sample = jnp.transpose(new_state_t, (1, 0, 2))[None]

    heads = lambda a: a.reshape(1, db, ts, N_HEADS, HEAD_DIM)
    return (y_p, y_s, k_prompt, v_prompt, logf_prompt, conv_prompt,
            heads(k_s), heads(v_s), logf_s.reshape(1, db, ts, N_HEADS), conv_sample)
```

```python
import functools

import numpy as np
import jax
import jax.numpy as jnp
from jax import lax
from jax.experimental import pallas as pl
from jax.experimental.pallas import tpu as pltpu

D_MODEL = 1024
D_ATT = 512
D_CONV = 512
N_HEADS = 8
HEAD_DIM = 64
CONV_WIDTH = 31
CONV_STATE = CONV_WIDTH - 1
D_PLE = 256
PAGE_SIZE = 128
EPS = 1e-6
ATT_SCALE = HEAD_DIM ** -0.5
LOG2E = 1.4426950408889634
Q_PRESCALE = ATT_SCALE * LOG2E
NEG_BIG = -1e30

LANES = 128
SUBLANES = 8
BF16_SUBLANES = 16
F_PAD = LANES

VMEM_LIMIT = 56 * 1024 * 1024

F32 = jnp.float32
BF16 = jnp.bfloat16
NT_DIMS = (((1,), (1,)), ((), ()))


def _silu(x):
    return x * jax.nn.sigmoid(x)


def _cparams(sem):
    return pltpu.CompilerParams(dimension_semantics=sem, vmem_limit_bytes=VMEM_LIMIT)


def _rms_normed(x_ref, g_ref):
    x = x_ref[...]
    ms = jnp.mean(x * x, axis=-1, keepdims=True)
    return ((x * lax.rsqrt(ms + EPS)) * g_ref[...]).astype(BF16)


PW_Q, PW_V, PW_GA, PW_GLUA, PW_GLUB, PW_GC = 0, 512, 1024, 1536, 2048, 2560
PW_COLS = 3072


def _proj_prompt_kernel(x_ref, g_ref, w_ref, wkt_ref, wft_ref, bf_ref,
                        qb_ref, vb_ref, vf_ref, ktb_ref, ktf_ref, logft_ref, ct_ref, sga_ref, u_ref, sgc_ref,
                        carry_ref, *, tm):
    xn = _rms_normed(x_ref, g_ref)

    def proj(lo, width):
        return jnp.dot(xn, w_ref[:, lo:lo + width], preferred_element_type=F32)

    qb_ref[...] = (proj(PW_Q, D_ATT) * Q_PRESCALE).astype(BF16)
    v = proj(PW_V, D_ATT)
    vf_ref[...] = v
    vb_ref[...] = v.astype(BF16)
    sga_ref[...] = _silu(proj(PW_GA, D_ATT))
    u_ref[...] = proj(PW_GLUA, D_CONV) * jax.nn.sigmoid(proj(PW_GLUB, D_CONV))
    sgc_ref[...] = _silu(proj(PW_GC, D_CONV))

    kt = lax.dot_general(wkt_ref[...], xn, NT_DIMS, preferred_element_type=F32)
    ktf_ref[...] = kt
    ktb_ref[...] = kt.astype(BF16)
    fzt = lax.dot_general(wft_ref[...], xn, NT_DIMS, preferred_element_type=F32)
    logft = jax.nn.log_sigmoid(fzt + bf_ref[...])[:N_HEADS, :]
    logft_ref[...] = logft

    @pl.when(pl.program_id(1) == 0)
    def _():
        carry_ref[...] = jnp.zeros_like(carry_ref)

    lane = lax.broadcasted_iota(jnp.int32, (N_HEADS, tm), 1)
    c = logft
    shift = 1
    while shift < tm:
        c = c + jnp.where(lane >= shift, pltpu.roll(c, shift, axis=1), 0.0)
        shift *= 2
    c = c + carry_ref[:, 0:1]
    ct_ref[...] = c
    carry_ref[...] = jnp.broadcast_to(c[:, tm - 1:tm], carry_ref.shape)


def _proj_prompt(x, g_norm, w_tok, wkt, wft, bf_col, *, tm=512):
    b, t, _ = x.shape
    tok = lambda d: pl.BlockSpec((None, tm, d), lambda i, j: (i, j, 0))
    tr = lambda r: pl.BlockSpec((None, r, tm), lambda i, j: (i, 0, j))
    const = lambda shape: pl.BlockSpec(shape, lambda i, j: (0,) * len(shape))
    out_shape = (
        jax.ShapeDtypeStruct((b, t, D_ATT), BF16),
        jax.ShapeDtypeStruct((b, t, D_ATT), BF16),
        jax.ShapeDtypeStruct((b, t, D_ATT), F32),
        jax.ShapeDtypeStruct((b, D_ATT, t), BF16),
        jax.ShapeDtypeStruct((b, D_ATT, t), F32),
        jax.ShapeDtypeStruct((b, N_HEADS, t), F32),
        jax.ShapeDtypeStruct((b, N_HEADS, t), F32),
        jax.ShapeDtypeStruct((b, t, D_ATT), F32),
        jax.ShapeDtypeStruct((b, t, D_CONV), F32),
        jax.ShapeDtypeStruct((b, t, D_CONV), F32),
    )
    out_specs = (tok(D_ATT), tok(D_ATT), tok(D_ATT), tr(D_ATT), tr(D_ATT), tr(N_HEADS), tr(N_HEADS),
                 tok(D_ATT), tok(D_CONV), tok(D_CONV))
    return pl.pallas_call(
        functools.partial(_proj_prompt_kernel, tm=tm),
        grid=(b, t // tm),
        in_specs=[tok(D_MODEL), const((1, D_MODEL)), const(w_tok.shape), const(wkt.shape), const(wft.shape),
                  const(bf_col.shape)],
        out_specs=out_specs,
        out_shape=out_shape,
        scratch_shapes=[pltpu.VMEM((N_HEADS, LANES), F32)],
        compiler_params=_cparams(("arbitrary", "arbitrary")),
        name="proj_prompt",
    )(x, g_norm, w_tok, wkt, wft, bf_col)


SW_Q, SW_K, SW_V, SW_GA, SW_GLUA, SW_GLUB, SW_GC, SW_F = 0, 512, 1024, 1536, 2048, 2560, 3072, 3584
SW_COLS = SW_F + F_PAD


def _proj_sample_kernel(x_ref, g_ref, w_ref, bf_ref, qb_ref, k_ref, v_ref, logf_ref, sga_ref, u_ref, sgc_ref):
    xn = _rms_normed(x_ref, g_ref)

    def proj(lo, width):
        return jnp.dot(xn, w_ref[:, lo:lo + width], preferred_element_type=F32)

    qb_ref[...] = (proj(SW_Q, D_ATT) * ATT_SCALE).astype(BF16)
    k_ref[...] = proj(SW_K, D_ATT)
    v_ref[...] = proj(SW_V, D_ATT)
    sga_ref[...] = _silu(proj(SW_GA, D_ATT))
    u_ref[...] = proj(SW_GLUA, D_CONV) * jax.nn.sigmoid(proj(SW_GLUB, D_CONV))
    sgc_ref[...] = _silu(proj(SW_GC, D_CONV))
    logf_ref[...] = jax.nn.log_sigmoid(proj(SW_F, F_PAD) + bf_ref[...])[:, :N_HEADS]


def _proj_sample(x, g_norm, w_tok, bf_row):
    n = x.shape[0]
    full = lambda shape: pl.BlockSpec(shape, lambda i: (0,) * len(shape))
    wide = jax.ShapeDtypeStruct((n, D_ATT), F32)
    out_shape = (jax.ShapeDtypeStruct((n, D_ATT), BF16), wide, wide, jax.ShapeDtypeStruct((n, N_HEADS), F32),
                 wide, wide, wide)
    return pl.pallas_call(
        _proj_sample_kernel,
        grid=(1,),
        in_specs=[full(x.shape), full(g_norm.shape), full(w_tok.shape), full(bf_row.shape)],
        out_specs=tuple(full(s.shape) for s in out_shape),
        out_shape=out_shape,
        compiler_params=_cparams(("arbitrary",)),
        name="proj_sample",
    )(x, g_norm, w_tok, bf_row)


CONV_HALO = 32


def _ln_swish(y, lg_ref, lb_ref):
    mu = jnp.mean(y, axis=-1, keepdims=True)
    yc = y - mu
    var = jnp.mean(yc * yc, axis=-1, keepdims=True)
    return _silu(yc * lax.rsqrt(var + EPS) * lg_ref[...] + lb_ref[...])


def _conv_prompt_kernel(u_ref, sgc_ref, w_ref, b_ref, lg_ref, lb_ref, o_ref, ext_ref, shift_ref, *, tc, rc):
    @pl.when(pl.program_id(1) == 0)
    def _():
        ext_ref[0:CONV_HALO, :] = jnp.zeros((CONV_HALO, D_CONV), F32)

    ext_ref[CONV_HALO:CONV_HALO + tc, :] = u_ref[...]
    off = CONV_HALO - CONV_STATE

    def chunk(ci, slot):
        r0 = pl.multiple_of(ci * rc, rc)
        win = ext_ref[pl.ds(r0, rc + CONV_HALO), :]
        span = rc + CONV_HALO - SUBLANES
        for r in range(1, SUBLANES):
            shift_ref[slot, r - 1] = win[r:r + span, :]
        acc = jnp.zeros((rc, D_CONV), F32)
        for j in range(CONV_WIDTH):
            r = (off + j) % SUBLANES
            a = off + j - r
            tap = win[a:a + rc, :] if r == 0 else shift_ref[slot, r - 1, a:a + rc, :]
            acc = acc + tap * w_ref[j:j + 1, :]
        o_ref[pl.ds(r0, rc), :] = _ln_swish(acc + b_ref[...], lg_ref, lb_ref) * sgc_ref[pl.ds(r0, rc), :]

    def chunk_pair(cp, carry):
        chunk(2 * cp, 0)
        chunk(2 * cp + 1, 1)
        return carry

    lax.fori_loop(0, tc // (2 * rc), chunk_pair, 0)
    ext_ref[0:CONV_HALO, :] = ext_ref[tc:tc + CONV_HALO, :]


def _conv_prompt(u, sgc, w_dw, b_dw, ln_g, ln_b, *, tc=512, rc=32):
    b, t, _ = u.shape
    tok = pl.BlockSpec((None, tc, D_CONV), lambda i, j: (i, j, 0))
    const = lambda shape: pl.BlockSpec(shape, lambda i, j: (0,) * len(shape))
    return pl.pallas_call(
        functools.partial(_conv_prompt_kernel, tc=tc, rc=rc),
        grid=(b, t // tc),
        in_specs=[tok, tok, const((CONV_WIDTH, D_CONV)), const((1, D_CONV)), const((1, D_CONV)),
                  const((1, D_CONV))],
        out_specs=tok,
        out_shape=jax.ShapeDtypeStruct((b, t, D_CONV), F32),
        scratch_shapes=[pltpu.VMEM((tc + CONV_HALO, D_CONV), F32),
                        pltpu.VMEM((2, SUBLANES - 1, rc + CONV_HALO - SUBLANES, D_CONV), F32)],
        compiler_params=_cparams(("arbitrary", "arbitrary")),
        name="conv_prompt",
    )(u, sgc, w_dw, b_dw, ln_g, ln_b)


def _conv_sample_kernel(st_ref, u_ref, sgc_ref, w_ref, b_ref, lg_ref, lb_ref, o_ref, ns_ref, *, ts):
    def ext(i):
        return st_ref[i] if i < CONV_STATE else u_ref[:, i - CONV_STATE, :]

    for t in range(ts):
        acc = jnp.zeros((o_ref.shape[0], D_CONV), F32)
        for j in range(CONV_WIDTH):
            acc = acc + ext(t + j) * w_ref[j:j + 1, :]
        o_ref[:, t, :] = _ln_swish(acc + b_ref[...], lg_ref, lb_ref) * sgc_ref[:, t, :]
    for i in range(CONV_STATE):
        ns_ref[i] = ext(i + ts)


def _conv_sample(state_t, u, sgc, w_dw, b_dw, ln_g, ln_b, *, bs=32):
    db, ts, _ = u.shape
    seq = pl.BlockSpec((bs, ts, D_CONV), lambda i: (i, 0, 0))
    st = pl.BlockSpec((CONV_STATE, bs, D_CONV), lambda i: (0, i, 0))
    const = lambda shape: pl.BlockSpec(shape, lambda i: (0,) * len(shape))
    return pl.pallas_call(
        functools.partial(_conv_sample_kernel, ts=ts),
        grid=(db // bs,),
        in_specs=[st, seq, seq, const((CONV_WIDTH, D_CONV)), const((1, D_CONV)), const((1, D_CONV)),
                  const((1, D_CONV))],
        out_specs=(seq, st),
        out_shape=(jax.ShapeDtypeStruct((db, ts, D_CONV), F32),
                   jax.ShapeDtypeStruct((CONV_STATE, db, D_CONV), F32)),
        compiler_params=_cparams(("arbitrary",)),
        name="conv_sample",
    )(state_t, u, sgc, w_dw, b_dw, ln_g, ln_b)


def _attn_prompt_kernel(q_ref, kt_ref, v_ref, ct_ref, o_ref, s0_ref, mblk0_ref, alpha0_ref,
                        s1_ref, mblk1_ref, alpha1_ref, m_ref, acc_ref, *, tq, rq):
    pair = pl.program_id(1)
    i = pl.program_id(2)
    q0 = pl.multiple_of(i * tq, tq)
    q2 = q_ref[...]
    lane = lax.broadcasted_iota(jnp.int32, (tq, LANES), 1)
    in_head = (lane < HEAD_DIM, lane >= HEAD_DIM)
    qh = [jnp.where(in_head[hh], q2, jnp.zeros_like(q2)) for hh in range(2)]
    head_row = lax.broadcasted_iota(jnp.int32, (N_HEADS, tq), 0)
    chains = [(hh, r0) for hh in range(2) for r0 in range(0, tq, rq)]

    def c_rows(k0):
        blk = ct_ref[:, pl.ds(k0, tq)] * LOG2E
        return [jnp.sum(jnp.where(head_row == 2 * pair + hh, blk, 0.0), axis=0, keepdims=True) for hh in range(2)]

    c_first = [r[:, 0:1] for r in c_rows(q0)]
    m_ref[...] = jnp.full_like(m_ref, NEG_BIG)
    acc_ref[...] = jnp.zeros_like(acc_ref)

    slots = ((s0_ref, mblk0_ref, alpha0_ref), (s1_ref, mblk1_ref, alpha1_ref))

    def scores(c, slot, masked, ktb, bias):
        s_ref, mblk_ref, alpha_ref = slots[slot]
        hh, r0 = chains[c]
        s = jnp.dot(qh[hh][r0:r0 + rq, :], ktb, preferred_element_type=F32) + bias[hh]
        if masked:
            row = lax.broadcasted_iota(jnp.int32, (rq, tq), 0) + r0
            col = lax.broadcasted_iota(jnp.int32, (rq, tq), 1)
            s = jnp.where(row >= col, s, NEG_BIG)
        m_prev = m_ref[c]
        m_new = jnp.maximum(m_prev, jnp.max(s, axis=1, keepdims=True))
        s_ref[c] = s
        mblk_ref[c] = m_new
        alpha_ref[c] = jnp.exp2(m_prev - m_new)
        m_ref[c] = m_new

    def accumulate(c, slot, vh):
        s_ref, mblk_ref, alpha_ref = slots[slot]
        hh, _ = chains[c]
        m_blk = mblk_ref[c]
        p = jnp.concatenate([jnp.exp2(s_ref[c, :, k * LANES:(k + 1) * LANES] - m_blk)
                             for k in range(tq // LANES)], axis=1)
        acc_ref[c] = alpha_ref[c] * acc_ref[c] + jnp.dot(p.astype(BF16), vh[hh], preferred_element_type=F32)

    def k_side(j):
        k0 = pl.multiple_of(j * tq, tq)
        c_keys = c_rows(k0)
        return kt_ref[:, pl.ds(k0, tq)], [c_first[hh] - c_keys[hh] for hh in range(2)]

    def v_side(j):
        vb = v_ref[pl.ds(pl.multiple_of(j * tq, tq), tq), :]
        return [jnp.where(in_head[hh], vb, jnp.ones_like(vb)) for hh in range(2)]

    def step(j, wslot, jprev):
        ktb, bias = k_side(j)
        vh = v_side(jprev)
        for c in range(len(chains)):
            scores(c, wslot, False, ktb, bias)
            accumulate(c, 1 - wslot, vh)

    def drain(slot, jlast):
        vh = v_side(jlast)
        for c in range(len(chains)):
            accumulate(c, slot, vh)

    ktb, bias = k_side(i)
    for c in range(len(chains)):
        scores(c, 0, True, ktb, bias)

    def two_steps(jj, carry):
        j = 2 * jj
        step(j, 1, jnp.where(jj == 0, i, j - 1))
        step(j + 1, 0, j)
        return carry

    lax.fori_loop(0, i // 2, two_steps, 0)

    @pl.when(i % 2 == 1)
    def _():
        step(i - 1, 1, jnp.where(i == 1, i, i - 2))
        drain(1, i - 1)

    @pl.when(i % 2 == 0)
    def _():
        drain(0, jnp.where(i == 0, i, i - 1))

    def head_out(hh):
        acc = jnp.concatenate([acc_ref[c] for c in range(len(chains)) if chains[c][0] == hh], axis=0)
        return acc / pltpu.roll(acc, HEAD_DIM, axis=1)

    o_ref[...] = jnp.where(in_head[0], head_out(0), head_out(1))


def _attn_prompt(qb, ktb, vb, ct, *, tq=512, rq=256):
    b, t, _ = qb.shape
    nchain = 2 * (tq // rq)
    return pl.pallas_call(
        functools.partial(_attn_prompt_kernel, tq=tq, rq=rq),
        grid=(b, N_HEADS // 2, t // tq),
        in_specs=[pl.BlockSpec((None, tq, LANES), lambda bi, pi, qi: (bi, qi, pi)),
                  pl.BlockSpec((None, LANES, t), lambda bi, pi, qi: (bi, pi, 0)),
                  pl.BlockSpec((None, t, LANES), lambda bi, pi, qi: (bi, 0, pi)),
                  pl.BlockSpec((None, N_HEADS, t), lambda bi, pi, qi: (bi, 0, 0))],
        out_specs=pl.BlockSpec((None, tq, LANES), lambda bi, pi, qi: (bi, qi, pi)),
        out_shape=jax.ShapeDtypeStruct((b, t, D_ATT), F32),
        scratch_shapes=[pltpu.VMEM((nchain, rq, tq), F32),
                        pltpu.VMEM((nchain, rq, LANES), F32),
                        pltpu.VMEM((nchain, rq, LANES), F32)] * 2
                       + [pltpu.VMEM((nchain, rq, LANES), F32),
                          pltpu.VMEM((nchain, rq, LANES), F32)],
        compiler_params=_cparams(("arbitrary", "arbitrary", "arbitrary")),
        name="attn_prompt",
    )(qb, ktb, vb, ct)


def _suffix_matrix():
    after = (np.arange(PAGE_SIZE)[:, None] > np.arange(PAGE_SIZE)[None, :]).astype(np.float32)
    return np.concatenate([after, np.ones_like(after)], axis=1)


def _attn_sample_kernel(pt_ref, smat_ref, q_ref, kn_ref, vn_ref, fn_ref, *refs, pp, ts, ns):
    page_refs = refs[:3 * pp * ns]
    o_ref = refs[3 * pp * ns]
    qrows_ref, m_ref, l_ref, acc_ref, after_ref = refs[3 * pp * ns + 1:]
    g = pl.program_id(1)
    nrow = ts * N_HEADS
    row = lax.broadcasted_iota(jnp.int32, (nrow, D_ATT), 0)
    lane = lax.broadcasted_iota(jnp.int32, (nrow, D_ATT), 1)
    own_head = (lane // HEAD_DIM) == (row % N_HEADS)

    @pl.when(g == 0)
    def _():
        for sq in range(ns):
            q = q_ref[sq].astype(F32)
            qb = jnp.broadcast_to(q[:, None, :], (ts, N_HEADS, D_ATT)).reshape(nrow, D_ATT)
            qrows_ref[sq] = jnp.where(own_head, qb, 0.0)
        m_ref[...] = jnp.full_like(m_ref, NEG_BIG)
        l_ref[...] = jnp.zeros_like(l_ref)
        acc_ref[...] = jnp.zeros_like(acc_ref)
        after_ref[...] = jnp.zeros_like(after_ref)

    smat = smat_ref[...]
    for sq in range(ns):
        f_refs = page_refs[(3 * sq) * pp:(3 * sq + 1) * pp]
        kt_refs = page_refs[(3 * sq + 1) * pp:(3 * sq + 2) * pp]
        vt_refs = page_refs[(3 * sq + 2) * pp:(3 * sq + 3) * pp]
        lhs = jnp.concatenate([f_refs[p][...] for p in range(pp)], axis=0)
        hi = lhs.astype(BF16)
        r1 = lhs - hi.astype(F32)
        mid = r1.astype(BF16)
        lo = (r1 - mid.astype(F32)).astype(BF16)
        sums = (jnp.dot(hi, smat, preferred_element_type=F32) + jnp.dot(mid, smat, preferred_element_type=F32)
                + jnp.dot(lo, smat, preferred_element_type=F32))
        after = after_ref[sq]
        bias = [None] * pp
        for p in reversed(range(pp)):
            blk = sums[p * N_HEADS:(p + 1) * N_HEADS, :]
            bias[p] = blk[:, :PAGE_SIZE] + after
            after = after + blk[:, PAGE_SIZE:]
        after_ref[sq] = after

        qrows = qrows_ref[sq]
        scores = []
        for p in range(pp):
            s = jnp.dot(qrows, kt_refs[p][...], preferred_element_type=F32)
            s = s.reshape(ts, N_HEADS, PAGE_SIZE) + bias[p][None, :, :]
            scores.append(s.reshape(nrow, PAGE_SIZE))
        s_all = jnp.concatenate(scores, axis=1)
        m_prev = m_ref[sq]
        m_new = jnp.maximum(m_prev, jnp.max(s_all, axis=1, keepdims=True))
        alpha = jnp.exp(m_prev - m_new)
        prob = jnp.exp(s_all - m_new)
        l_ref[sq] = alpha * l_ref[sq] + jnp.sum(prob, axis=1, keepdims=True)
        pv = jnp.zeros((nrow, D_ATT), F32)
        for p in range(pp):
            pv = pv + lax.dot_general(prob[:, p * PAGE_SIZE:(p + 1) * PAGE_SIZE], vt_refs[p][...], NT_DIMS,
                                      preferred_element_type=F32)
        acc_ref[sq] = alpha * acc_ref[sq] + pv
        m_ref[sq] = m_new

    @pl.when(g == pl.num_programs(1) - 1)
    def _():
        for sq in range(ns):
            _attn_sample_new_keys(sq, kn_ref, vn_ref, fn_ref, o_ref, qrows_ref, m_ref, l_ref, acc_ref, own_head, ts)


def _attn_sample_new_keys(sq, kn_ref, vn_ref, fn_ref, o_ref, qrows_ref, m_ref, l_ref, acc_ref, own_head, ts):
    nrow = ts * N_HEADS
    qrows = qrows_ref[sq]
    kn = kn_ref[sq].astype(BF16).astype(F32)
    vn = vn_ref[sq].astype(BF16).astype(F32)
    fn = fn_ref[sq]
    eye = (lax.broadcasted_iota(jnp.int32, (N_HEADS, N_HEADS), 0)
           == lax.broadcasted_iota(jnp.int32, (N_HEADS, N_HEADS), 1))
    tok = lax.broadcasted_iota(jnp.int32, (nrow, 1), 0) // N_HEADS
    logits = []
    cn = jnp.zeros((1, N_HEADS), F32)
    for j in range(ts):
        cn = cn + fn[j:j + 1, :]
        cn_col = jnp.sum(jnp.where(eye, jnp.broadcast_to(cn, (N_HEADS, N_HEADS)), 0.0), axis=1, keepdims=True)
        cn_rows = jnp.broadcast_to(cn_col[None, :, :], (ts, N_HEADS, 1)).reshape(nrow, 1)
        sj = jnp.sum(qrows * kn[j:j + 1, :], axis=1, keepdims=True) - cn_rows
        logits.append(jnp.where(tok >= j, sj, NEG_BIG))
    m_prev = m_ref[sq]
    m_fin = m_prev
    for sj in logits:
        m_fin = jnp.maximum(m_fin, sj)
    alpha = jnp.exp(m_prev - m_fin)
    l_fin = alpha * l_ref[sq]
    acc = alpha * acc_ref[sq]
    for j, sj in enumerate(logits):
        pj = jnp.exp(sj - m_fin)
        l_fin = l_fin + pj
        acc = acc + pj * vn[j:j + 1, :]
    out = jnp.where(own_head, acc / l_fin, 0.0)
    o_ref[sq] = jnp.sum(out.reshape(ts, N_HEADS, D_ATT), axis=1)


def _attn_sample(q, kn, vn, fn, logf_pages_t, k_pages_t, v_pages_t, page_table, *, pp=8, ns=2):
    db, ts, _ = q.shape
    n_pages = page_table.shape[1]
    n_groups = n_pages // pp
    nrow = ts * N_HEADS
    smat = jnp.asarray(_suffix_matrix(), dtype=BF16)
    seq = lambda r, d: pl.BlockSpec((ns, r, d), lambda bi, gi, pt: (bi, 0, 0))

    def page_spec(rows, sq, p):
        return pl.BlockSpec((None, rows, PAGE_SIZE),
                            lambda bi, gi, pt: (pt[bi * ns + sq, (n_groups - 1 - gi) * pp + p], 0, 0))

    page_specs, page_args = [], []
    for sq in range(ns):
        for rows, arr in ((N_HEADS, logf_pages_t), (D_ATT, k_pages_t), (D_ATT, v_pages_t)):
            page_specs += [page_spec(rows, sq, p) for p in range(pp)]
            page_args += [arr] * pp
    grid_spec = pltpu.PrefetchScalarGridSpec(
        num_scalar_prefetch=1,
        grid=(db // ns, n_groups),
        in_specs=[pl.BlockSpec(smat.shape, lambda bi, gi, pt: (0, 0)),
                  seq(ts, D_ATT), seq(ts, D_ATT), seq(ts, D_ATT), seq(ts, N_HEADS)] + page_specs,
        out_specs=seq(ts, D_ATT),
        scratch_shapes=[pltpu.VMEM((ns, nrow, D_ATT), F32), pltpu.VMEM((ns, nrow, 1), F32),
                        pltpu.VMEM((ns, nrow, 1), F32), pltpu.VMEM((ns, nrow, D_ATT), F32),
                        pltpu.VMEM((ns, N_HEADS, PAGE_SIZE), F32)],
    )
    return pl.pallas_call(
        functools.partial(_attn_sample_kernel, pp=pp, ts=ts, ns=ns),
        grid_spec=grid_spec,
        out_shape=jax.ShapeDtypeStruct((db, ts, D_ATT), F32),
        compiler_params=_cparams(("arbitrary", "arbitrary")),
        name="attn_sample",
    )(page_table, smat, q, kn, vn, fn, *page_args)


def _out_kernel(att_ref, sga_ref, mixc_ref, x_ref, p_ref, wo_ref, wpe_ref, gpe_ref, wpg_ref, gf_ref, y_ref):
    mix_a = (att_ref[...] * sga_ref[...]).astype(BF16)
    mix_c = mixc_ref[...].astype(BF16)
    h = (x_ref[...] + jnp.dot(mix_a, wo_ref[0:D_ATT, :], preferred_element_type=F32)
         + jnp.dot(mix_c, wo_ref[D_ATT:, :], preferred_element_type=F32))
    pe = jnp.dot(p_ref[...].astype(BF16), wpe_ref[...], preferred_element_type=F32)
    e = pe * lax.rsqrt(jnp.mean(pe * pe, axis=-1, keepdims=True) + EPS) * gpe_ref[...]
    gate = jax.nn.sigmoid(jnp.dot(h.astype(BF16), wpg_ref[...], preferred_element_type=F32))
    h = h + gate * e
    y_ref[...] = h * lax.rsqrt(jnp.mean(h * h, axis=-1, keepdims=True) + EPS) * gf_ref[...]


def _out_stage(att, sga, mixc, x, p, w_out, w_pe, g_pe, w_pg, g_final, *, tm):
    n = x.shape[0]
    tok = lambda d: pl.BlockSpec((tm, d), lambda i: (i, 0))
    const = lambda shape: pl.BlockSpec(shape, lambda i: (0,) * len(shape))
    return pl.pallas_call(
        _out_kernel,
        grid=(n // tm,),
        in_specs=[tok(D_ATT), tok(D_ATT), tok(D_CONV), tok(D_MODEL), tok(D_PLE),
                  const((D_ATT + D_CONV, D_MODEL)), const((D_PLE, D_MODEL)), const((1, D_MODEL)),
                  const((D_MODEL, D_MODEL)), const((1, D_MODEL))],
        out_specs=tok(D_MODEL),
        out_shape=jax.ShapeDtypeStruct((n, D_MODEL), F32),
        compiler_params=_cparams(("arbitrary",)),
        name="out_stage",
    )(att, sga, mixc, x, p, w_out, w_pe, g_pe, w_pg, g_final)


def _in_weights(w_in, b_f):
    a, c, h = D_ATT, D_CONV, N_HEADS
    w = w_in.astype(BF16)
    q, k, v = w[:, 0:a], w[:, a:2 * a], w[:, 2 * a:3 * a]
    f = w[:, 3 * a:3 * a + h]
    o = 3 * a + h
    rest = w[:, o:]
    w_prompt = jnp.concatenate([q, v, rest], axis=1)
    wkt = k.T
    wft = jnp.pad(f.T, ((0, BF16_SUBLANES - h), (0, 0)))
    bf_col = jnp.pad(b_f, (0, BF16_SUBLANES - h)).reshape(BF16_SUBLANES, 1)
    w_sample = jnp.concatenate([q, k, v, rest, jnp.pad(f, ((0, 0), (0, F_PAD - h)))], axis=1)
    bf_row = jnp.pad(b_f, (0, F_PAD - h)).reshape(1, F_PAD)
    return w_prompt, wkt, wft, bf_col, w_sample, bf_row


def kernel(x_prompt, x_sample, cache_k, cache_v, cache_logf, state_conv, page_table, p_prompt, p_sample,
           g_norm, w_in, b_f, w_dw, b_dw, ln_g, ln_b, w_out, w_pe, g_pe, w_pg, g_final):
    assert w_in.shape[0] == 1, "single-layer step"
    b, t, _ = x_prompt.shape
    db, ts, _ = x_sample.shape
    n_phys = cache_k.shape[1]
    row = lambda a: a.reshape(1, -1)

    w_prompt, wkt, wft, bf_col, w_sample, bf_row = _in_weights(w_in[0], b_f[0])
    wo, wpe, wpg = w_out[0].astype(BF16), w_pe[0].astype(BF16), w_pg[0].astype(BF16)
    gn, gpe, gf = row(g_norm[0]), row(g_pe[0]), row(g_final)
    bdw, lng, lnb = row(b_dw[0]), row(ln_g[0]), row(ln_b[0])

    qb, vb, v_p, ktb, kt_p, logft_p, ct, sga_p, u_p, sgc_p = _proj_prompt(x_prompt, gn, w_prompt, wkt, wft, bf_col)
    mixc_p = _conv_prompt(u_p, sgc_p, w_dw[0], bdw, lng, lnb)
    att_p = _attn_prompt(qb, ktb, vb, ct)
    n_p = b * t
    y_p = _out_stage(att_p.reshape(n_p, D_ATT), sga_p.reshape(n_p, D_ATT), mixc_p.reshape(n_p, D_CONV),
                     x_prompt.reshape(n_p, D_MODEL), p_prompt[0].reshape(n_p, D_PLE),
                     wo, wpe, gpe, wpg, gf, tm=512).reshape(b, t, D_MODEL)
    k_prompt = jnp.transpose(kt_p.reshape(b, N_HEADS, HEAD_DIM, t), (0, 3, 1, 2))[None]
    v_prompt = v_p.reshape(1, b, t, N_HEADS, HEAD_DIM)
    logf_prompt = jnp.transpose(logft_p, (0, 2, 1))[None]
    conv_prompt = u_p[:, t - CONV_STATE:, :][None]

    n_s = db * ts
    qb_s, k_s, v_s, logf_s, sga_s, u_s, sgc_s = _proj_sample(x_sample.reshape(n_s, D_MODEL), gn, w_sample, bf_row)
    seqs = lambda a: a.reshape(db, ts, a.shape[-1])
    state_t = jnp.transpose(state_conv[0], (1, 0, 2))
    mixc_s, new_state_t = _conv_sample(state_t, seqs(u_s), seqs(sgc_s), w_dw[0], bdw, lng, lnb)
    k_pages_t = jnp.transpose(cache_k[0], (0, 2, 3, 1)).reshape(n_phys, D_ATT, PAGE_SIZE)
    v_pages_t = jnp.transpose(cache_v[0], (0, 2, 3, 1)).reshape(n_phys, D_ATT, PAGE_SIZE)
    logf_pages_t = jnp.transpose(cache_logf[0], (0, 2, 1))
    att_s = _attn_sample(seqs(qb_s), seqs(k_s), seqs(v_s), seqs(logf_s), logf_pages_t, k_pages_t, v_pages_t,
                         page_table)
    y_s = _out_stage(att_s.reshape(n_s, D_ATT), sga_s, mixc_s.reshape(n_s, D_CONV),
                     x_sample.reshape(n_s, D_MODEL), p_sample[0].reshape(n_s, D_PLE),
                     wo, wpe, gpe, wpg, gf, tm=n_s).reshape(db, ts, D_MODEL)
    conv_sample = jnp.transpose(new_state_t, (1, 0, 2))[None]

    heads = lambda a: a.reshape(1, db, ts, N_HEADS, HEAD_DIM)
    return (y_p, y_s, k_prompt, v_prompt, logf_prompt, conv_prompt,
            heads(k_s), heads(v_s), logf_s.reshape(1, db, ts, N_HEADS), conv_sample)
```

```python
import functools

import numpy as np
import jax
import jax.numpy as jnp
from jax import lax
from jax.experimental import pallas as pl
from jax.experimental.pallas import tpu as pltpu

D_MODEL = 1024
D_ATT = 512
D_CONV = 512
N_HEADS = 8
HEAD_DIM = 64
CONV_WIDTH = 31
CONV_STATE = CONV_WIDTH - 1
D_PLE = 256
PAGE_SIZE = 128
EPS = 1e-6
ATT_SCALE = HEAD_DIM ** -0.5
LOG2E = 1.4426950408889634
Q_PRESCALE = ATT_SCALE * LOG2E
NEG_BIG = -1e30

LANES = 128
SUBLANES = 8
BF16_SUBLANES = 16
F_PAD = LANES

VMEM_LIMIT = 56 * 1024 * 1024

F32 = jnp.float32
BF16 = jnp.bfloat16
NT_DIMS = (((1,), (1,)), ((), ()))


def _silu(x):
    return x * jax.nn.sigmoid(x)


def _cparams(sem):
    return pltpu.CompilerParams(dimension_semantics=sem, vmem_limit_bytes=VMEM_LIMIT)


def _rms_normed(x_ref, g_ref):
    x = x_ref[...]
    ms = jnp.mean(x * x, axis=-1, keepdims=True)
    return ((x * lax.rsqrt(ms + EPS)) * g_ref[...]).astype(BF16)


PW_Q, PW_V, PW_GA, PW_GLUA, PW_GLUB, PW_GC = 0, 512, 1024, 1536, 2048, 2560
PW_COLS = 3072


CONV_HALO = 32


def _ln_swish(y, lg_ref, lb_ref):
    mu = jnp.mean(y, axis=-1, keepdims=True)
    yc = y - mu
    var = jnp.mean(yc * yc, axis=-1, keepdims=True)
    return _silu(yc * lax.rsqrt(var + EPS) * lg_ref[...] + lb_ref[...])


def _proj_prompt_kernel(x_ref, g_ref, w_ref, wkt_ref, wft_ref, bf_ref, wdw_ref, bdw_ref, lg_ref, lb_ref,
                        qb_ref, vb_ref, vf_ref, ktb_ref, ktf_ref, logft_ref, ct_ref, sga_ref, mixc_ref, tail_ref,
                        carry_ref, ext_ref, shift_ref, gate_ref, *, tm, rc):
    xn = _rms_normed(x_ref, g_ref)

    def proj(lo, width):
        return jnp.dot(xn, w_ref[:, lo:lo + width], preferred_element_type=F32)

    @pl.when(pl.program_id(1) == 0)
    def _():
        carry_ref[...] = jnp.zeros_like(carry_ref)
        ext_ref[0:CONV_HALO, :] = jnp.zeros((CONV_HALO, D_CONV), F32)

    ext_ref[CONV_HALO:CONV_HALO + tm, :] = proj(PW_GLUA, D_CONV) * jax.nn.sigmoid(proj(PW_GLUB, D_CONV))
    gate_ref[...] = _silu(proj(PW_GC, D_CONV))
    tail_ref[...] = ext_ref[tm:tm + CONV_HALO, :]

    def section_q():
        qb_ref[...] = (proj(PW_Q, D_ATT) * Q_PRESCALE).astype(BF16)

    def section_v():
        v = proj(PW_V, D_ATT)
        vf_ref[...] = v
        vb_ref[...] = v.astype(BF16)

    def section_ga():
        sga_ref[...] = _silu(proj(PW_GA, D_ATT))

    def section_k():
        kt = lax.dot_general(wkt_ref[...], xn, NT_DIMS, preferred_element_type=F32)
        ktf_ref[...] = kt
        ktb_ref[...] = kt.astype(BF16)

    def section_f():
        fzt = lax.dot_general(wft_ref[...], xn, NT_DIMS, preferred_element_type=F32)
        logft = jax.nn.log_sigmoid(fzt + bf_ref[...])[:N_HEADS, :]
        logft_ref[...] = logft
        lane = lax.broadcasted_iota(jnp.int32, (N_HEADS, tm), 1)
        c = logft
        shift = 1
        while shift < tm:
            c = c + jnp.where(lane >= shift, pltpu.roll(c, shift, axis=1), 0.0)
            shift *= 2
        c = c + carry_ref[:, 0:1]
        ct_ref[...] = c
        carry_ref[...] = jnp.broadcast_to(c[:, tm - 1:tm], carry_ref.shape)

    off = CONV_HALO - CONV_STATE
    span = rc + CONV_HALO - SUBLANES

    def conv_chunk(ci):
        r0 = ci * rc
        slot = ci % 2
        win = ext_ref[r0:r0 + rc + CONV_HALO, :]
        for r in range(1, SUBLANES):
            shift_ref[slot, r - 1] = win[r:r + span, :]
        acc = jnp.zeros((rc, D_CONV), F32)
        for j in range(CONV_WIDTH):
            r = (off + j) % SUBLANES
            a = off + j - r
            tap = win[a:a + rc, :] if r == 0 else shift_ref[slot, r - 1, a:a + rc, :]
            acc = acc + tap * wdw_ref[j:j + 1, :]
        mixc_ref[r0:r0 + rc, :] = _ln_swish(acc + bdw_ref[...], lg_ref, lb_ref) * gate_ref[r0:r0 + rc, :]

    sections = [section_q, section_v, section_ga, section_k, section_f]
    nchunk = tm // rc
    bounds = [nchunk * n // len(sections) for n in range(len(sections) + 1)]
    for n, section in enumerate(sections):
        section()
        for ci in range(bounds[n], bounds[n + 1]):
            conv_chunk(ci)
    ext_ref[0:CONV_HALO, :] = ext_ref[tm:tm + CONV_HALO, :]


def _proj_prompt(x, g_norm, w_tok, wkt, wft, bf_col, w_dw, b_dw, ln_g, ln_b, *, tm=512, rc=32):
    b, t, _ = x.shape
    tok = lambda d: pl.BlockSpec((None, tm, d), lambda i, j: (i, j, 0))
    tr = lambda r: pl.BlockSpec((None, r, tm), lambda i, j: (i, 0, j))
    const = lambda shape: pl.BlockSpec(shape, lambda i, j: (0,) * len(shape))
    out_shape = (
        jax.ShapeDtypeStruct((b, t, D_ATT), BF16),
        jax.ShapeDtypeStruct((b, t, D_ATT), BF16),
        jax.ShapeDtypeStruct((b, t, D_ATT), F32),
        jax.ShapeDtypeStruct((b, D_ATT, t), BF16),
        jax.ShapeDtypeStruct((b, D_ATT, t), F32),
        jax.ShapeDtypeStruct((b, N_HEADS, t), F32),
        jax.ShapeDtypeStruct((b, N_HEADS, t), F32),
        jax.ShapeDtypeStruct((b, t, D_ATT), F32),
        jax.ShapeDtypeStruct((b, t, D_CONV), F32),
        jax.ShapeDtypeStruct((b, CONV_HALO, D_CONV), F32),
    )
    out_specs = (tok(D_ATT), tok(D_ATT), tok(D_ATT), tr(D_ATT), tr(D_ATT), tr(N_HEADS), tr(N_HEADS),
                 tok(D_ATT), tok(D_CONV), pl.BlockSpec((None, CONV_HALO, D_CONV), lambda i, j: (i, 0, 0)))
    return pl.pallas_call(
        functools.partial(_proj_prompt_kernel, tm=tm, rc=rc),
        grid=(b, t // tm),
        in_specs=[tok(D_MODEL), const((1, D_MODEL)), const(w_tok.shape), const(wkt.shape), const(wft.shape),
                  const(bf_col.shape), const((CONV_WIDTH, D_CONV)), const((1, D_CONV)), const((1, D_CONV)),
                  const((1, D_CONV))],
        out_specs=out_specs,
        out_shape=out_shape,
        scratch_shapes=[pltpu.VMEM((N_HEADS, LANES), F32),
                        pltpu.VMEM((tm + CONV_HALO, D_CONV), F32),
                        pltpu.VMEM((2, SUBLANES - 1, rc + CONV_HALO - SUBLANES, D_CONV), F32),
                        pltpu.VMEM((tm, D_CONV), F32)],
        compiler_params=_cparams(("arbitrary", "arbitrary")),
        name="proj_prompt",
    )(x, g_norm, w_tok, wkt, wft, bf_col, w_dw, b_dw, ln_g, ln_b)


SW_Q, SW_K, SW_V, SW_GA, SW_GLUA, SW_GLUB, SW_GC, SW_F = 0, 512, 1024, 1536, 2048, 2560, 3072, 3584
SW_COLS = SW_F + F_PAD


def _proj_sample_kernel(x_ref, g_ref, w_ref, bf_ref, qb_ref, k_ref, v_ref, logf_ref, sga_ref, u_ref, sgc_ref):
    xn = _rms_normed(x_ref, g_ref)

    def proj(lo, width):
        return jnp.dot(xn, w_ref[:, lo:lo + width], preferred_element_type=F32)

    qb_ref[...] = (proj(SW_Q, D_ATT) * ATT_SCALE).astype(BF16)
    k_ref[...] = proj(SW_K, D_ATT)
    v_ref[...] = proj(SW_V, D_ATT)
    sga_ref[...] = _silu(proj(SW_GA, D_ATT))
    u_ref[...] = proj(SW_GLUA, D_CONV) * jax.nn.sigmoid(proj(SW_GLUB, D_CONV))
    sgc_ref[...] = _silu(proj(SW_GC, D_CONV))
    logf_ref[...] = jax.nn.log_sigmoid(proj(SW_F, F_PAD) + bf_ref[...])[:, :N_HEADS]


def _proj_sample(x, g_norm, w_tok, bf_row):
    n = x.shape[0]
    full = lambda shape: pl.BlockSpec(shape, lambda i: (0,) * len(shape))
    wide = jax.ShapeDtypeStruct((n, D_ATT), F32)
    out_shape = (jax.ShapeDtypeStruct((n, D_ATT), BF16), wide, wide, jax.ShapeDtypeStruct((n, N_HEADS), F32),
                 wide, wide, wide)
    return pl.pallas_call(
        _proj_sample_kernel,
        grid=(1,),
        in_specs=[full(x.shape), full(g_norm.shape), full(w_tok.shape), full(bf_row.shape)],
        out_specs=tuple(full(s.shape) for s in out_shape),
        out_shape=out_shape,
        compiler_params=_cparams(("arbitrary",)),
        name="proj_sample",
    )(x, g_norm, w_tok, bf_row)


def _conv_sample_kernel(st_ref, u_ref, sgc_ref, w_ref, b_ref, lg_ref, lb_ref, o_ref, ns_ref, *, ts):
    def ext(i):
        return st_ref[i] if i < CONV_STATE else u_ref[:, i - CONV_STATE, :]

    for t in range(ts):
        acc = jnp.zeros((o_ref.shape[0], D_CONV), F32)
        for j in range(CONV_WIDTH):
            acc = acc + ext(t + j) * w_ref[j:j + 1, :]
        o_ref[:, t, :] = _ln_swish(acc + b_ref[...], lg_ref, lb_ref) * sgc_ref[:, t, :]
    for i in range(CONV_STATE):
        ns_ref[i] = ext(i + ts)


def _conv_sample(state_t, u, sgc, w_dw, b_dw, ln_g, ln_b, *, bs=32):
    db, ts, _ = u.shape
    seq = pl.BlockSpec((bs, ts, D_CONV), lambda i: (i, 0, 0))
    st = pl.BlockSpec((CONV_STATE, bs, D_CONV), lambda i: (0, i, 0))
    const = lambda shape: pl.BlockSpec(shape, lambda i: (0,) * len(shape))
    return pl.pallas_call(
        functools.partial(_conv_sample_kernel, ts=ts),
        grid=(db // bs,),
        in_specs=[st, seq, seq, const((CONV_WIDTH, D_CONV)), const((1, D_CONV)), const((1, D_CONV)),
                  const((1, D_CONV))],
        out_specs=(seq, st),
        out_shape=(jax.ShapeDtypeStruct((db, ts, D_CONV), F32),
                   jax.ShapeDtypeStruct((CONV_STATE, db, D_CONV), F32)),
        compiler_params=_cparams(("arbitrary",)),
        name="conv_sample",
    )(state_t, u, sgc, w_dw, b_dw, ln_g, ln_b)


def _attn_prompt_kernel(q_ref, kt_ref, v_ref, ct_ref, o_ref, s0_ref, mblk0_ref, alpha0_ref,
                        s1_ref, mblk1_ref, alpha1_ref, m_ref, acc_ref, *, tq, rq):
    pair = pl.program_id(1)
    i = pl.program_id(2)
    q0 = pl.multiple_of(i * tq, tq)
    q2 = q_ref[...]
    lane = lax.broadcasted_iota(jnp.int32, (tq, LANES), 1)
    in_head = (lane < HEAD_DIM, lane >= HEAD_DIM)
    qh = [jnp.where(in_head[hh], q2, jnp.zeros_like(q2)) for hh in range(2)]
    head_row = lax.broadcasted_iota(jnp.int32, (N_HEADS, tq), 0)
    chains = [(hh, r0) for hh in range(2) for r0 in range(0, tq, rq)]

    def c_rows(k0):
        blk = ct_ref[:, pl.ds(k0, tq)] * LOG2E
        return [jnp.sum(jnp.where(head_row == 2 * pair + hh, blk, 0.0), axis=0, keepdims=True) for hh in range(2)]

    c_first = [r[:, 0:1] for r in c_rows(q0)]
    m_ref[...] = jnp.full_like(m_ref, NEG_BIG)
    acc_ref[...] = jnp.zeros_like(acc_ref)

    slots = ((s0_ref, mblk0_ref, alpha0_ref), (s1_ref, mblk1_ref, alpha1_ref))

    def scores(c, slot, masked, ktb, bias):
        s_ref, mblk_ref, alpha_ref = slots[slot]
        hh, r0 = chains[c]
        s = jnp.dot(qh[hh][r0:r0 + rq, :], ktb, preferred_element_type=F32) + bias[hh]
        if masked:
            row = lax.broadcasted_iota(jnp.int32, (rq, tq), 0) + r0
            col = lax.broadcasted_iota(jnp.int32, (rq, tq), 1)
            s = jnp.where(row >= col, s, NEG_BIG)
        m_prev = m_ref[c]
        m_new = jnp.maximum(m_prev, jnp.max(s, axis=1, keepdims=True))
        s_ref[c] = s
        mblk_ref[c] = m_new
        alpha_ref[c] = jnp.exp2(m_prev - m_new)
        m_ref[c] = m_new

    def accumulate(c, slot, vh):
        s_ref, mblk_ref, alpha_ref = slots[slot]
        hh, _ = chains[c]
        m_blk = mblk_ref[c]
        p = jnp.concatenate([jnp.exp2(s_ref[c, :, k * LANES:(k + 1) * LANES] - m_blk)
                             for k in range(tq // LANES)], axis=1)
        acc_ref[c] = alpha_ref[c] * acc_ref[c] + jnp.dot(p.astype(BF16), vh[hh], preferred_element_type=F32)

    def k_side(j):
        k0 = pl.multiple_of(j * tq, tq)
        c_keys = c_rows(k0)
        return kt_ref[:, pl.ds(k0, tq)], [c_first[hh] - c_keys[hh] for hh in range(2)]

    def v_side(j):
        vb = v_ref[pl.ds(pl.multiple_of(j * tq, tq), tq), :]
        return [jnp.where(in_head[hh], vb, jnp.ones_like(vb)) for hh in range(2)]

    def step(j, wslot, jprev):
        ktb, bias = k_side(j)
        vh = v_side(jprev)
        for c in range(len(chains)):
            scores(c, wslot, False, ktb, bias)
            accumulate(c, 1 - wslot, vh)

    def drain(slot, jlast):
        vh = v_side(jlast)
        for c in range(len(chains)):
            accumulate(c, slot, vh)

    ktb, bias = k_side(i)
    for c in range(len(chains)):
        scores(c, 0, True, ktb, bias)

    def two_steps(jj, carry):
        j = 2 * jj
        step(j, 1, jnp.where(jj == 0, i, j - 1))
        step(j + 1, 0, j)
        return carry

    lax.fori_loop(0, i // 2, two_steps, 0)

    @pl.when(i % 2 == 1)
    def _():
        step(i - 1, 1, jnp.where(i == 1, i, i - 2))
        drain(1, i - 1)

    @pl.when(i % 2 == 0)
    def _():
        drain(0, jnp.where(i == 0, i, i - 1))

    def head_out(hh):
        acc = jnp.concatenate([acc_ref[c] for c in range(len(chains)) if chains[c][0] == hh], axis=0)
        return acc / pltpu.roll(acc, HEAD_DIM, axis=1)

    o_ref[...] = jnp.where(in_head[0], head_out(0), head_out(1))


def _attn_prompt(qb, ktb, vb, ct, *, tq=512, rq=256):
    b, t, _ = qb.shape
    nchain = 2 * (tq // rq)
    return pl.pallas_call(
        functools.partial(_attn_prompt_kernel, tq=tq, rq=rq),
        grid=(b, N_HEADS // 2, t // tq),
        in_specs=[pl.BlockSpec((None, tq, LANES), lambda bi, pi, qi: (bi, qi, pi)),
                  pl.BlockSpec((None, LANES, t), lambda bi, pi, qi: (bi, pi, 0)),
                  pl.BlockSpec((None, t, LANES), lambda bi, pi, qi: (bi, 0, pi)),
                  pl.BlockSpec((None, N_HEADS, t), lambda bi, pi, qi: (bi, 0, 0))],
        out_specs=pl.BlockSpec((None, tq, LANES), lambda bi, pi, qi: (bi, qi, pi)),
        out_shape=jax.ShapeDtypeStruct((b, t, D_ATT), F32),
        scratch_shapes=[pltpu.VMEM((nchain, rq, tq), F32),
                        pltpu.VMEM((nchain, rq, LANES), F32),
                        pltpu.VMEM((nchain, rq, LANES), F32)] * 2
                       + [pltpu.VMEM((nchain, rq, LANES), F32),
                          pltpu.VMEM((nchain, rq, LANES), F32)],
        compiler_params=_cparams(("arbitrary", "arbitrary", "arbitrary")),
        name="attn_prompt",
    )(qb, ktb, vb, ct)


def _suffix_matrix():
    after = (np.arange(PAGE_SIZE)[:, None] > np.arange(PAGE_SIZE)[None, :]).astype(np.float32)
    return np.concatenate([after, np.ones_like(after)], axis=1)


def _attn_sample_kernel(pt_ref, smat_ref, q_ref, kn_ref, vn_ref, fn_ref, *refs, pp, ts, ns):
    page_refs = refs[:3 * pp * ns]
    o_ref = refs[3 * pp * ns]
    qrows_ref, m_ref, l_ref, acc_ref, after_ref = refs[3 * pp * ns + 1:]
    g = pl.program_id(1)
    nrow = ts * N_HEADS
    row = lax.broadcasted_iota(jnp.int32, (nrow, D_ATT), 0)
    lane = lax.broadcasted_iota(jnp.int32, (nrow, D_ATT), 1)
    own_head = (lane // HEAD_DIM) == (row % N_HEADS)

    @pl.when(g == 0)
    def _():
        for sq in range(ns):
            q = q_ref[sq].astype(F32)
            qb = jnp.broadcast_to(q[:, None, :], (ts, N_HEADS, D_ATT)).reshape(nrow, D_ATT)
            qrows_ref[sq] = jnp.where(own_head, qb, 0.0)
        m_ref[...] = jnp.full_like(m_ref, NEG_BIG)
        l_ref[...] = jnp.zeros_like(l_ref)
        acc_ref[...] = jnp.zeros_like(acc_ref)
        after_ref[...] = jnp.zeros_like(after_ref)

    smat = smat_ref[...]
    for sq in range(ns):
        f_refs = page_refs[(3 * sq) * pp:(3 * sq + 1) * pp]
        kt_refs = page_refs[(3 * sq + 1) * pp:(3 * sq + 2) * pp]
        vt_refs = page_refs[(3 * sq + 2) * pp:(3 * sq + 3) * pp]
        lhs = jnp.concatenate([f_refs[p][...] for p in range(pp)], axis=0)
        hi = lhs.astype(BF16)
        r1 = lhs - hi.astype(F32)
        mid = r1.astype(BF16)
        lo = (r1 - mid.astype(F32)).astype(BF16)
        sums = (jnp.dot(hi, smat, preferred_element_type=F32) + jnp.dot(mid, smat, preferred_element_type=F32)
                + jnp.dot(lo, smat, preferred_element_type=F32))
        after = after_ref[sq]
        bias = [None] * pp
        for p in reversed(range(pp)):
            blk = sums[p * N_HEADS:(p + 1) * N_HEADS, :]
            bias[p] = blk[:, :PAGE_SIZE] + after
            after = after + blk[:, PAGE_SIZE:]
        after_ref[sq] = after

        qrows = qrows_ref[sq]
        scores = []
        for p in range(pp):
            s = jnp.dot(qrows, kt_refs[p][...], preferred_element_type=F32)
            s = s.reshape(ts, N_HEADS, PAGE_SIZE) + bias[p][None, :, :]
            scores.append(s.reshape(nrow, PAGE_SIZE))
        s_all = jnp.concatenate(scores, axis=1)
        m_prev = m_ref[sq]
        m_new = jnp.maximum(m_prev, jnp.max(s_all, axis=1, keepdims=True))
        alpha = jnp.exp(m_prev - m_new)
        prob = jnp.exp(s_all - m_new)
        l_ref[sq] = alpha * l_ref[sq] + jnp.sum(prob, axis=1, keepdims=True)
        pv = jnp.zeros((nrow, D_ATT), F32)
        for p in range(pp):
            pv = pv + lax.dot_general(prob[:, p * PAGE_SIZE:(p + 1) * PAGE_SIZE], vt_refs[p][...], NT_DIMS,
                                      preferred_element_type=F32)
        acc_ref[sq] = alpha * acc_ref[sq] + pv
        m_ref[sq] = m_new

    @pl.when(g == pl.num_programs(1) - 1)
    def _():
        for sq in range(ns):
            _attn_sample_new_keys(sq, kn_ref, vn_ref, fn_ref, o_ref, qrows_ref, m_ref, l_ref, acc_ref, own_head, ts)


def _attn_sample_new_keys(sq, kn_ref, vn_ref, fn_ref, o_ref, qrows_ref, m_ref, l_ref, acc_ref, own_head, ts):
    nrow = ts * N_HEADS
    qrows = qrows_ref[sq]
    kn = kn_ref[sq].astype(BF16).astype(F32)
    vn = vn_ref[sq].astype(BF16).astype(F32)
    fn = fn_ref[sq]
    eye = (lax.broadcasted_iota(jnp.int32, (N_HEADS, N_HEADS), 0)
           == lax.broadcasted_iota(jnp.int32, (N_HEADS, N_HEADS), 1))
    tok = lax.broadcasted_iota(jnp.int32, (nrow, 1), 0) // N_HEADS
    logits = []
    cn = jnp.zeros((1, N_HEADS), F32)
    for j in range(ts):
        cn = cn + fn[j:j + 1, :]
        cn_col = jnp.sum(jnp.where(eye, jnp.broadcast_to(cn, (N_HEADS, N_HEADS)), 0.0), axis=1, keepdims=True)
        cn_rows = jnp.broadcast_to(cn_col[None, :, :], (ts, N_HEADS, 1)).reshape(nrow, 1)
        sj = jnp.sum(qrows * kn[j:j + 1, :], axis=1, keepdims=True) - cn_rows
        logits.append(jnp.where(tok >= j, sj, NEG_BIG))
    m_prev = m_ref[sq]
    m_fin = m_prev
    for sj in logits:
        m_fin = jnp.maximum(m_fin, sj)
    alpha = jnp.exp(m_prev - m_fin)
    l_fin = alpha * l_ref[sq]
    acc = alpha * acc_ref[sq]
    for j, sj in enumerate(logits):
        pj = jnp.exp(sj - m_fin)
        l_fin = l_fin + pj
        acc = acc + pj * vn[j:j + 1, :]
    out = jnp.where(own_head, acc / l_fin, 0.0)
    o_ref[sq] = jnp.sum(out.reshape(ts, N_HEADS, D_ATT), axis=1)


def _attn_sample(q, kn, vn, fn, logf_pages_t, k_pages_t, v_pages_t, page_table, *, pp=8, ns=2):
    db, ts, _ = q.shape
    n_pages = page_table.shape[1]
    n_groups = n_pages // pp
    nrow = ts * N_HEADS
    smat = jnp.asarray(_suffix_matrix(), dtype=BF16)
    seq = lambda r, d: pl.BlockSpec((ns, r, d), lambda bi, gi, pt: (bi, 0, 0))

    def page_spec(rows, sq, p):
        return pl.BlockSpec((None, rows, PAGE_SIZE),
                            lambda bi, gi, pt: (pt[bi * ns + sq, (n_groups - 1 - gi) * pp + p], 0, 0))

    page_specs, page_args = [], []
    for sq in range(ns):
        for rows, arr in ((N_HEADS, logf_pages_t), (D_ATT, k_pages_t), (D_ATT, v_pages_t)):
            page_specs += [page_spec(rows, sq, p) for p in range(pp)]
            page_args += [arr] * pp
    grid_spec = pltpu.PrefetchScalarGridSpec(
        num_scalar_prefetch=1,
        grid=(db // ns, n_groups),
        in_specs=[pl.BlockSpec(smat.shape, lambda bi, gi, pt: (0, 0)),
                  seq(ts, D_ATT), seq(ts, D_ATT), seq(ts, D_ATT), seq(ts, N_HEADS)] + page_specs,
        out_specs=seq(ts, D_ATT),
        scratch_shapes=[pltpu.VMEM((ns, nrow, D_ATT), F32), pltpu.VMEM((ns, nrow, 1), F32),
                        pltpu.VMEM((ns, nrow, 1), F32), pltpu.VMEM((ns, nrow, D_ATT), F32),
                        pltpu.VMEM((ns, N_HEADS, PAGE_SIZE), F32)],
    )
    return pl.pallas_call(
        functools.partial(_attn_sample_kernel, pp=pp, ts=ts, ns=ns),
        grid_spec=grid_spec,
        out_shape=jax.ShapeDtypeStruct((db, ts, D_ATT), F32),
        compiler_params=_cparams(("arbitrary", "arbitrary")),
        name="attn_sample",
    )(page_table, smat, q, kn, vn, fn, *page_args)


def _out_kernel(att_ref, sga_ref, mixc_ref, x_ref, p_ref, wo_ref, wpe_ref, gpe_ref, wpg_ref, gf_ref, y_ref):
    mix_a = (att_ref[...] * sga_ref[...]).astype(BF16)
    mix_c = mixc_ref[...].astype(BF16)
    h = (x_ref[...] + jnp.dot(mix_a, wo_ref[0:D_ATT, :], preferred_element_type=F32)
         + jnp.dot(mix_c, wo_ref[D_ATT:, :], preferred_element_type=F32))
    pe = jnp.dot(p_ref[...].astype(BF16), wpe_ref[...], preferred_element_type=F32)
    e = pe * lax.rsqrt(jnp.mean(pe * pe, axis=-1, keepdims=True) + EPS) * gpe_ref[...]
    gate = jax.nn.sigmoid(jnp.dot(h.astype(BF16), wpg_ref[...], preferred_element_type=F32))
    h = h + gate * e
    y_ref[...] = h * lax.rsqrt(jnp.mean(h * h, axis=-1, keepdims=True) + EPS) * gf_ref[...]


def _out_stage(att, sga, mixc, x, p, w_out, w_pe, g_pe, w_pg, g_final, *, tm):
    n = x.shape[0]
    tok = lambda d: pl.BlockSpec((tm, d), lambda i: (i, 0))
    const = lambda shape: pl.BlockSpec(shape, lambda i: (0,) * len(shape))
    return pl.pallas_call(
        _out_kernel,
        grid=(n // tm,),
        in_specs=[tok(D_ATT), tok(D_ATT), tok(D_CONV), tok(D_MODEL), tok(D_PLE),
                  const((D_ATT + D_CONV, D_MODEL)), const((D_PLE, D_MODEL)), const((1, D_MODEL)),
                  const((D_MODEL, D_MODEL)), const((1, D_MODEL))],
        out_specs=tok(D_MODEL),
        out_shape=jax.ShapeDtypeStruct((n, D_MODEL), F32),
        compiler_params=_cparams(("arbitrary",)),
        name="out_stage",
    )(att, sga, mixc, x, p, w_out, w_pe, g_pe, w_pg, g_final)


def _in_weights(w_in, b_f):
    a, c, h = D_ATT, D_CONV, N_HEADS
    w = w_in.astype(BF16)
    q, k, v = w[:, 0:a], w[:, a:2 * a], w[:, 2 * a:3 * a]
    f = w[:, 3 * a:3 * a + h]
    o = 3 * a + h
    rest = w[:, o:]
    w_prompt = jnp.concatenate([q, v, rest], axis=1)
    wkt = k.T
    wft = jnp.pad(f.T, ((0, BF16_SUBLANES - h), (0, 0)))
    bf_col = jnp.pad(b_f, (0, BF16_SUBLANES - h)).reshape(BF16_SUBLANES, 1)
    w_sample = jnp.concatenate([q, k, v, rest, jnp.pad(f, ((0, 0), (0, F_PAD - h)))], axis=1)
    bf_row = jnp.pad(b_f, (0, F_PAD - h)).reshape(1, F_PAD)
    return w_prompt, wkt, wft, bf_col, w_sample, bf_row


def kernel(x_prompt, x_sample, cache_k, cache_v, cache_logf, state_conv, page_table, p_prompt, p_sample,
           g_norm, w_in, b_f, w_dw, b_dw, ln_g, ln_b, w_out, w_pe, g_pe, w_pg, g_final):
    assert w_in.shape[0] == 1, "single-layer step"
    b, t, _ = x_prompt.shape
    db, ts, _ = x_sample.shape
    n_phys = cache_k.shape[1]
    row = lambda a: a.reshape(1, -1)

    w_prompt, wkt, wft, bf_col, w_sample, bf_row = _in_weights(w_in[0], b_f[0])
    wo, wpe, wpg = w_out[0].astype(BF16), w_pe[0].astype(BF16), w_pg[0].astype(BF16)
    gn, gpe, gf = row(g_norm[0]), row(g_pe[0]), row(g_final)
    bdw, lng, lnb = row(b_dw[0]), row(ln_g[0]), row(ln_b[0])

    qb, vb, v_p, ktb, kt_p, logft_p, ct, sga_p, mixc_p, u_tail = _proj_prompt(
        x_prompt, gn, w_prompt, wkt, wft, bf_col, w_dw[0], bdw, lng, lnb)
    att_p = _attn_prompt(qb, ktb, vb, ct)
    n_p = b * t
    y_p = _out_stage(att_p.reshape(n_p, D_ATT), sga_p.reshape(n_p, D_ATT), mixc_p.reshape(n_p, D_CONV),
                     x_prompt.reshape(n_p, D_MODEL), p_prompt[0].reshape(n_p, D_PLE),
                     wo, wpe, gpe, wpg, gf, tm=512).reshape(b, t, D_MODEL)
    k_prompt = jnp.transpose(kt_p.reshape(b, N_HEADS, HEAD_DIM, t), (0, 3, 1, 2))[None]
    v_prompt = v_p.reshape(1, b, t, N_HEADS, HEAD_DIM)
    logf_prompt = jnp.transpose(logft_p, (0, 2, 1))[None]
    conv_prompt = u_tail[:, CONV_HALO - CONV_STATE:, :][None]

    n_s = db * ts
    qb_s, k_s, v_s, logf_s, sga_s, u_s, sgc_s = _proj_sample(x_sample.reshape(n_s, D_MODEL), gn, w_sample, bf_row)
    seqs = lambda a: a.reshape(db, ts, a.shape[-1])
    state_t = jnp.transpose(state_conv[0], (1, 0, 2))
    mixc_s, new_state_t = _conv_sample(state_t, seqs(u_s), seqs(sgc_s), w_dw[0], bdw, lng, lnb)
    k_pages_t = jnp.transpose(cache_k[0], (0, 2, 3, 1)).reshape(n_phys, D_ATT, PAGE_SIZE)
    v_pages_t = jnp.transpose(cache_v[0], (0, 2, 3, 1)).reshape(n_phys, D_ATT, PAGE_SIZE)
    logf_pages_t = jnp.transpose(cache_logf[0], (0, 2, 1))
    att_s = _attn_sample(seqs(qb_s), seqs(k_s), seqs(v_s), seqs(logf_s), logf_pages_t, k_pages_t, v_pages_t,
                         page_table)
    y_s = _out_stage(att_s.reshape(n_s, D_ATT), sga_s, mixc_s.reshape(n_s, D_CONV),
                     x_sample.reshape(n_s, D_MODEL), p_sample[0].reshape(n_s, D_PLE),
                     wo, wpe, gpe, wpg, gf, tm=n_s).reshape(db, ts, D_MODEL)
    conv_sample = jnp.transpose(new_state_t, (1, 0, 2))[None]

    heads = lambda a: a.reshape(1, db, ts, N_HEADS, HEAD_DIM)
    return (y_p, y_s, k_prompt, v_prompt, logf_prompt, conv_prompt,
            heads(k_s), heads(v_s), logf_s.reshape(1, db, ts, N_HEADS), conv_sample)
```

```python
import functools

import numpy as np
import jax
import jax.numpy as jnp
from jax import lax
from jax.experimental import pallas as pl
from jax.experimental.pallas import tpu as pltpu

D_MODEL = 1024
D_ATT = 512
D_CONV = 512
N_HEADS = 8
HEAD_DIM = 64
CONV_WIDTH = 31
CONV_STATE = CONV_WIDTH - 1
D_PLE = 256
PAGE_SIZE = 128
EPS = 1e-6
ATT_SCALE = HEAD_DIM ** -0.5
LOG2E = 1.4426950408889634
Q_PRESCALE = ATT_SCALE * LOG2E
NEG_BIG = -1e30

LANES = 128
SUBLANES = 8
BF16_SUBLANES = 16
F_PAD = LANES

VMEM_LIMIT = 56 * 1024 * 1024

F32 = jnp.float32
BF16 = jnp.bfloat16
NT_DIMS = (((1,), (1,)), ((), ()))


def _silu(x):
    return x * jax.nn.sigmoid(x)


def _cparams(sem):
    return pltpu.CompilerParams(dimension_semantics=sem, vmem_limit_bytes=VMEM_LIMIT)


def _rms_normed(x_ref, g_ref):
    x = x_ref[...]
    ms = jnp.mean(x * x, axis=-1, keepdims=True)
    return ((x * lax.rsqrt(ms + EPS)) * g_ref[...]).astype(BF16)


PW_Q, PW_V, PW_GA, PW_GLUA, PW_GLUB, PW_GC = 0, 512, 1024, 1536, 2048, 2560
PW_COLS = 3072


CONV_HALO = 32


def _ln_swish(y, lg_ref, lb_ref):
    mu = jnp.mean(y, axis=-1, keepdims=True)
    yc = y - mu
    var = jnp.mean(yc * yc, axis=-1, keepdims=True)
    return _silu(yc * lax.rsqrt(var + EPS) * lg_ref[...] + lb_ref[...])


def _proj_prompt_kernel(x_ref, g_ref, w_ref, wkt_ref, wft_ref, bf_ref, wdw_ref, bdw_ref, lg_ref, lb_ref,
                        qb_ref, vb_ref, vf_ref, ktb_ref, ktf_ref, logft_ref, ct_ref, sga_ref, mixc_ref, tail_ref,
                        carry_ref, ext_ref, shift_ref, gate_ref, *, tm, rc):
    xn = _rms_normed(x_ref, g_ref)

    def proj(lo, width):
        return jnp.dot(xn, w_ref[:, lo:lo + width], preferred_element_type=F32)

    @pl.when(pl.program_id(1) == 0)
    def _():
        carry_ref[...] = jnp.zeros_like(carry_ref)
        ext_ref[0:CONV_HALO, :] = jnp.zeros((CONV_HALO, D_CONV), F32)

    ext_ref[CONV_HALO:CONV_HALO + tm, :] = proj(PW_GLUA, D_CONV) * jax.nn.sigmoid(proj(PW_GLUB, D_CONV))
    gate_ref[...] = _silu(proj(PW_GC, D_CONV))
    tail_ref[...] = ext_ref[tm:tm + CONV_HALO, :]

    def section_q():
        qb_ref[...] = (proj(PW_Q, D_ATT) * Q_PRESCALE).astype(BF16)

    def section_v():
        v = proj(PW_V, D_ATT)
        vf_ref[...] = v
        vb_ref[...] = v.astype(BF16)

    def section_ga():
        sga_ref[...] = _silu(proj(PW_GA, D_ATT))

    def section_k():
        kt = lax.dot_general(wkt_ref[...], xn, NT_DIMS, preferred_element_type=F32)
        ktf_ref[...] = kt
        ktb_ref[...] = kt.astype(BF16)

    def section_f():
        fzt = lax.dot_general(wft_ref[...], xn, NT_DIMS, preferred_element_type=F32)
        logft = jax.nn.log_sigmoid(fzt + bf_ref[...])[:N_HEADS, :]
        logft_ref[...] = logft
        lane = lax.broadcasted_iota(jnp.int32, (N_HEADS, tm), 1)
        c = logft
        shift = 1
        while shift < tm:
            c = c + jnp.where(lane >= shift, pltpu.roll(c, shift, axis=1), 0.0)
            shift *= 2
        c = c + carry_ref[:, 0:1]
        ct_ref[...] = c
        carry_ref[...] = jnp.broadcast_to(c[:, tm - 1:tm], carry_ref.shape)

    off = CONV_HALO - CONV_STATE
    span = rc + CONV_HALO - SUBLANES

    def conv_chunk(ci):
        r0 = ci * rc
        slot = ci % 2
        win = ext_ref[r0:r0 + rc + CONV_HALO, :]
        for r in range(1, SUBLANES):
            shift_ref[slot, r - 1] = win[r:r + span, :]
        acc = jnp.zeros((rc, D_CONV), F32)
        for j in range(CONV_WIDTH):
            r = (off + j) % SUBLANES
            a = off + j - r
            tap = win[a:a + rc, :] if r == 0 else shift_ref[slot, r - 1, a:a + rc, :]
            acc = acc + tap * wdw_ref[j:j + 1, :]
        mixc_ref[r0:r0 + rc, :] = _ln_swish(acc + bdw_ref[...], lg_ref, lb_ref) * gate_ref[r0:r0 + rc, :]

    sections = [section_q, section_v, section_ga, section_k, section_f]
    nchunk = tm // rc
    bounds = [nchunk * n // len(sections) for n in range(len(sections) + 1)]
    for n, section in enumerate(sections):
        section()
        for ci in range(bounds[n], bounds[n + 1]):
            conv_chunk(ci)
    ext_ref[0:CONV_HALO, :] = ext_ref[tm:tm + CONV_HALO, :]


def _proj_prompt(x, g_norm, w_tok, wkt, wft, bf_col, w_dw, b_dw, ln_g, ln_b, *, tm=512, rc=32):
    b, t, _ = x.shape
    tok = lambda d: pl.BlockSpec((None, tm, d), lambda i, j: (i, j, 0))
    tr = lambda r: pl.BlockSpec((None, r, tm), lambda i, j: (i, 0, j))
    const = lambda shape: pl.BlockSpec(shape, lambda i, j: (0,) * len(shape))
    out_shape = (
        jax.ShapeDtypeStruct((b, t, D_ATT), BF16),
        jax.ShapeDtypeStruct((b, t, D_ATT), BF16),
        jax.ShapeDtypeStruct((b, t, D_ATT), F32),
        jax.ShapeDtypeStruct((b, D_ATT, t), BF16),
        jax.ShapeDtypeStruct((b, D_ATT, t), F32),
        jax.ShapeDtypeStruct((b, N_HEADS, t), F32),
        jax.ShapeDtypeStruct((b, N_HEADS, t), F32),
        jax.ShapeDtypeStruct((b, t, D_ATT), F32),
        jax.ShapeDtypeStruct((b, t, D_CONV), F32),
        jax.ShapeDtypeStruct((b, CONV_HALO, D_CONV), F32),
    )
    out_specs = (tok(D_ATT), tok(D_ATT), tok(D_ATT), tr(D_ATT), tr(D_ATT), tr(N_HEADS), tr(N_HEADS),
                 tok(D_ATT), tok(D_CONV), pl.BlockSpec((None, CONV_HALO, D_CONV), lambda i, j: (i, 0, 0)))
    return pl.pallas_call(
        functools.partial(_proj_prompt_kernel, tm=tm, rc=rc),
        grid=(b, t // tm),
        in_specs=[tok(D_MODEL), const((1, D_MODEL)), const(w_tok.shape), const(wkt.shape), const(wft.shape),
                  const(bf_col.shape), const((CONV_WIDTH, D_CONV)), const((1, D_CONV)), const((1, D_CONV)),
                  const((1, D_CONV))],
        out_specs=out_specs,
        out_shape=out_shape,
        scratch_shapes=[pltpu.VMEM((N_HEADS, LANES), F32),
                        pltpu.VMEM((tm + CONV_HALO, D_CONV), F32),
                        pltpu.VMEM((2, SUBLANES - 1, rc + CONV_HALO - SUBLANES, D_CONV), F32),
                        pltpu.VMEM((tm, D_CONV), F32)],
        compiler_params=_cparams(("arbitrary", "arbitrary")),
        name="proj_prompt",
    )(x, g_norm, w_tok, wkt, wft, bf_col, w_dw, b_dw, ln_g, ln_b)


SW_Q, SW_K, SW_V, SW_GA, SW_GLUA, SW_GLUB, SW_GC, SW_F = 0, 512, 1024, 1536, 2048, 2560, 3072, 3584
SW_COLS = SW_F + F_PAD


def _proj_sample_kernel(x_ref, g_ref, w_ref, bf_ref, qb_ref, k_ref, v_ref, logf_ref, sga_ref, u_ref, sgc_ref):
    xn = _rms_normed(x_ref, g_ref)

    def proj(lo, width):
        return jnp.dot(xn, w_ref[:, lo:lo + width], preferred_element_type=F32)

    qb_ref[...] = (proj(SW_Q, D_ATT) * ATT_SCALE).astype(BF16)
    k_ref[...] = proj(SW_K, D_ATT)
    v_ref[...] = proj(SW_V, D_ATT)
    sga_ref[...] = _silu(proj(SW_GA, D_ATT))
    u_ref[...] = proj(SW_GLUA, D_CONV) * jax.nn.sigmoid(proj(SW_GLUB, D_CONV))
    sgc_ref[...] = _silu(proj(SW_GC, D_CONV))
    logf_ref[...] = jax.nn.log_sigmoid(proj(SW_F, F_PAD) + bf_ref[...])[:, :N_HEADS]


def _proj_sample(x, g_norm, w_tok, bf_row):
    n = x.shape[0]
    full = lambda shape: pl.BlockSpec(shape, lambda i: (0,) * len(shape))
    wide = jax.ShapeDtypeStruct((n, D_ATT), F32)
    out_shape = (jax.ShapeDtypeStruct((n, D_ATT), BF16), wide, wide, jax.ShapeDtypeStruct((n, N_HEADS), F32),
                 wide, wide, wide)
    return pl.pallas_call(
        _proj_sample_kernel,
        grid=(1,),
        in_specs=[full(x.shape), full(g_norm.shape), full(w_tok.shape), full(bf_row.shape)],
        out_specs=tuple(full(s.shape) for s in out_shape),
        out_shape=out_shape,
        compiler_params=_cparams(("arbitrary",)),
        name="proj_sample",
    )(x, g_norm, w_tok, bf_row)


def _conv_sample_kernel(st_ref, u_ref, sgc_ref, w_ref, b_ref, lg_ref, lb_ref, o_ref, ns_ref, *, ts):
    def ext(i):
        return st_ref[i] if i < CONV_STATE else u_ref[:, i - CONV_STATE, :]

    for t in range(ts):
        acc = jnp.zeros((o_ref.shape[0], D_CONV), F32)
        for j in range(CONV_WIDTH):
            acc = acc + ext(t + j) * w_ref[j:j + 1, :]
        o_ref[:, t, :] = _ln_swish(acc + b_ref[...], lg_ref, lb_ref) * sgc_ref[:, t, :]
    for i in range(CONV_STATE):
        ns_ref[i] = ext(i + ts)


def _conv_sample(state_t, u, sgc, w_dw, b_dw, ln_g, ln_b, *, bs=32):
    db, ts, _ = u.shape
    seq = pl.BlockSpec((bs, ts, D_CONV), lambda i: (i, 0, 0))
    st = pl.BlockSpec((CONV_STATE, bs, D_CONV), lambda i: (0, i, 0))
    const = lambda shape: pl.BlockSpec(shape, lambda i: (0,) * len(shape))
    return pl.pallas_call(
        functools.partial(_conv_sample_kernel, ts=ts),
        grid=(db // bs,),
        in_specs=[st, seq, seq, const((CONV_WIDTH, D_CONV)), const((1, D_CONV)), const((1, D_CONV)),
                  const((1, D_CONV))],
        out_specs=(seq, st),
        out_shape=(jax.ShapeDtypeStruct((db, ts, D_CONV), F32),
                   jax.ShapeDtypeStruct((CONV_STATE, db, D_CONV), F32)),
        compiler_params=_cparams(("arbitrary",)),
        name="conv_sample",
    )(state_t, u, sgc, w_dw, b_dw, ln_g, ln_b)


def _attn_prompt_kernel(q_ref, kt_ref, v_ref, ct_ref, o_ref, s0_ref, mblk0_ref, alpha0_ref,
                        s1_ref, mblk1_ref, alpha1_ref, m_ref, acc_ref, *, tq, rq):
    pair = pl.program_id(1)
    i = pl.program_id(2)
    q0 = pl.multiple_of(i * tq, tq)
    q2 = q_ref[...]
    lane = lax.broadcasted_iota(jnp.int32, (tq, LANES), 1)
    in_head = (lane < HEAD_DIM, lane >= HEAD_DIM)
    qh = [jnp.where(in_head[hh], q2, jnp.zeros_like(q2)) for hh in range(2)]
    head_row = lax.broadcasted_iota(jnp.int32, (N_HEADS, tq), 0)
    chains = [(hh, r0) for hh in range(2) for r0 in range(0, tq, rq)]

    def c_rows(k0):
        blk = ct_ref[:, pl.ds(k0, tq)] * LOG2E
        return [jnp.sum(jnp.where(head_row == 2 * pair + hh, blk, 0.0), axis=0, keepdims=True) for hh in range(2)]

    c_first = [r[:, 0:1] for r in c_rows(q0)]
    m_ref[...] = jnp.full_like(m_ref, NEG_BIG)
    acc_ref[...] = jnp.zeros_like(acc_ref)

    slots = ((s0_ref, mblk0_ref, alpha0_ref), (s1_ref, mblk1_ref, alpha1_ref))

    def scores(c, slot, masked, ktb, bias):
        s_ref, mblk_ref, alpha_ref = slots[slot]
        hh, r0 = chains[c]
        s = jnp.dot(qh[hh][r0:r0 + rq, :], ktb, preferred_element_type=F32) + bias[hh]
        if masked:
            row = lax.broadcasted_iota(jnp.int32, (rq, tq), 0) + r0
            col = lax.broadcasted_iota(jnp.int32, (rq, tq), 1)
            s = jnp.where(row >= col, s, NEG_BIG)
        m_prev = m_ref[c]
        m_new = jnp.maximum(m_prev, jnp.max(s, axis=1, keepdims=True))
        s_ref[c] = s
        mblk_ref[c] = m_new
        alpha_ref[c] = jnp.exp2(m_prev - m_new)
        m_ref[c] = m_new

    def accumulate(c, slot, vh):
        s_ref, mblk_ref, alpha_ref = slots[slot]
        hh, _ = chains[c]
        m_blk = mblk_ref[c]
        p = jnp.concatenate([jnp.exp2(s_ref[c, :, k * LANES:(k + 1) * LANES] - m_blk)
                             for k in range(tq // LANES)], axis=1)
        acc_ref[c] = alpha_ref[c] * acc_ref[c] + jnp.dot(p.astype(BF16), vh[hh], preferred_element_type=F32)

    def k_side(j):
        k0 = pl.multiple_of(j * tq, tq)
        c_keys = c_rows(k0)
        return kt_ref[:, pl.ds(k0, tq)], [c_first[hh] - c_keys[hh] for hh in range(2)]

    def v_side(j):
        vb = v_ref[pl.ds(pl.multiple_of(j * tq, tq), tq), :]
        return [jnp.where(in_head[hh], vb, jnp.ones_like(vb)) for hh in range(2)]

    def step(j, wslot, jprev):
        ktb, bias = k_side(j)
        vh = v_side(jprev)
        for c in range(len(chains)):
            scores(c, wslot, False, ktb, bias)
            accumulate(c, 1 - wslot, vh)

    def drain(slot, jlast):
        vh = v_side(jlast)
        for c in range(len(chains)):
            accumulate(c, slot, vh)

    ktb, bias = k_side(i)
    for c in range(len(chains)):
        scores(c, 0, True, ktb, bias)

    def two_steps(jj, carry):
        j = 2 * jj
        step(j, 1, jnp.where(jj == 0, i, j - 1))
        step(j + 1, 0, j)
        return carry

    lax.fori_loop(0, i // 2, two_steps, 0)

    @pl.when(i % 2 == 1)
    def _():
        step(i - 1, 1, jnp.where(i == 1, i, i - 2))
        drain(1, i - 1)

    @pl.when(i % 2 == 0)
    def _():
        drain(0, jnp.where(i == 0, i, i - 1))

    def head_out(hh):
        acc = jnp.concatenate([acc_ref[c] for c in range(len(chains)) if chains[c][0] == hh], axis=0)
        return acc / pltpu.roll(acc, HEAD_DIM, axis=1)

    o_ref[...] = jnp.where(in_head[0], head_out(0), head_out(1))


def _attn_prompt(qb, ktb, vb, ct, *, tq=512, rq=256):
    b, t, _ = qb.shape
    nchain = 2 * (tq // rq)
    return pl.pallas_call(
        functools.partial(_attn_prompt_kernel, tq=tq, rq=rq),
        grid=(b, N_HEADS // 2, t // tq),
        in_specs=[pl.BlockSpec((None, tq, LANES), lambda bi, pi, qi: (bi, qi, pi)),
                  pl.BlockSpec((None, LANES, t), lambda bi, pi, qi: (bi, pi, 0)),
                  pl.BlockSpec((None, t, LANES), lambda bi, pi, qi: (bi, 0, pi)),
                  pl.BlockSpec((None, N_HEADS, t), lambda bi, pi, qi: (bi, 0, 0))],
        out_specs=pl.BlockSpec((None, tq, LANES), lambda bi, pi, qi: (bi, qi, pi)),
        out_shape=jax.ShapeDtypeStruct((b, t, D_ATT), F32),
        scratch_shapes=[pltpu.VMEM((nchain, rq, tq), F32),
                        pltpu.VMEM((nchain, rq, LANES), F32),
                        pltpu.VMEM((nchain, rq, LANES), F32)] * 2
                       + [pltpu.VMEM((nchain, rq, LANES), F32),
                          pltpu.VMEM((nchain, rq, LANES), F32)],
        compiler_params=_cparams(("arbitrary", "arbitrary", "arbitrary")),
        name="attn_prompt",
    )(qb, ktb, vb, ct)


def _suffix_matrix():
    after = (np.arange(PAGE_SIZE)[:, None] > np.arange(PAGE_SIZE)[None, :]).astype(np.float32)
    return np.concatenate([after, np.ones_like(after)], axis=1)


def _attn_sample_kernel(pt_ref, smat_ref, q_ref, kn_ref, vn_ref, fn_ref, f_hbm, kt_hbm, vt_hbm, o_ref,
                        fbuf, ktbuf, vtbuf, sems, qrows_ref, m_ref, l_ref, acc_ref, after_ref, *, pp, ts, ns):
    bi = pl.program_id(0)
    g = pl.program_id(1)
    n_groups = pl.num_programs(1)
    step = bi * n_groups + g
    cur = step % 2
    nrow = ts * N_HEADS

    def page_copies(b, grp, buf):
        copies = []
        for sq in range(ns):
            for p in range(pp):
                page = pt_ref[b * ns + sq, (n_groups - 1 - grp) * pp + p]
                for kind, (hbm, vmem) in enumerate(((f_hbm, fbuf), (kt_hbm, ktbuf), (vt_hbm, vtbuf))):
                    copies.append(pltpu.make_async_copy(hbm.at[page], vmem.at[buf, sq, p], sems.at[buf, kind]))
        return copies

    @pl.when(step == 0)
    def _():
        for cp in page_copies(bi, g, cur):
            cp.start()

    @pl.when(step + 1 < pl.num_programs(0) * n_groups)
    def _():
        wrap = g + 1 == n_groups
        for cp in page_copies(jnp.where(wrap, bi + 1, bi), jnp.where(wrap, 0, g + 1), 1 - cur):
            cp.start()

    for cp in page_copies(bi, g, cur):
        cp.wait()
    row = lax.broadcasted_iota(jnp.int32, (nrow, D_ATT), 0)
    lane = lax.broadcasted_iota(jnp.int32, (nrow, D_ATT), 1)
    own_head = (lane // HEAD_DIM) == (row % N_HEADS)

    @pl.when(g == 0)
    def _():
        for sq in range(ns):
            q = q_ref[sq].astype(F32)
            qb = jnp.broadcast_to(q[:, None, :], (ts, N_HEADS, D_ATT)).reshape(nrow, D_ATT)
            qrows_ref[sq] = jnp.where(own_head, qb, 0.0)
        m_ref[...] = jnp.full_like(m_ref, NEG_BIG)
        l_ref[...] = jnp.zeros_like(l_ref)
        acc_ref[...] = jnp.zeros_like(acc_ref)
        after_ref[...] = jnp.zeros_like(after_ref)

    smat = smat_ref[...]
    for sq in range(ns):
        lhs = fbuf[cur, sq].reshape(pp * N_HEADS, PAGE_SIZE)
        hi = lhs.astype(BF16)
        r1 = lhs - hi.astype(F32)
        mid = r1.astype(BF16)
        lo = (r1 - mid.astype(F32)).astype(BF16)
        sums = (jnp.dot(hi, smat, preferred_element_type=F32) + jnp.dot(mid, smat, preferred_element_type=F32)
                + jnp.dot(lo, smat, preferred_element_type=F32))
        after = after_ref[sq]
        bias = [None] * pp
        for p in reversed(range(pp)):
            blk = sums[p * N_HEADS:(p + 1) * N_HEADS, :]
            bias[p] = blk[:, :PAGE_SIZE] + after
            after = after + blk[:, PAGE_SIZE:]
        after_ref[sq] = after

        qrows = qrows_ref[sq]
        scores = []
        for p in range(pp):
            s = jnp.dot(qrows, ktbuf[cur, sq, p], preferred_element_type=F32)
            s = s.reshape(ts, N_HEADS, PAGE_SIZE) + bias[p][None, :, :]
            scores.append(s.reshape(nrow, PAGE_SIZE))
        s_all = jnp.concatenate(scores, axis=1)
        m_prev = m_ref[sq]
        m_new = jnp.maximum(m_prev, jnp.max(s_all, axis=1, keepdims=True))
        alpha = jnp.exp(m_prev - m_new)
        prob = jnp.exp(s_all - m_new)
        l_ref[sq] = alpha * l_ref[sq] + jnp.sum(prob, axis=1, keepdims=True)
        pv = jnp.zeros((nrow, D_ATT), F32)
        for p in range(pp):
            pv = pv + lax.dot_general(prob[:, p * PAGE_SIZE:(p + 1) * PAGE_SIZE], vtbuf[cur, sq, p], NT_DIMS,
                                      preferred_element_type=F32)
        acc_ref[sq] = alpha * acc_ref[sq] + pv
        m_ref[sq] = m_new

    @pl.when(g == pl.num_programs(1) - 1)
    def _():
        for sq in range(ns):
            _attn_sample_new_keys(sq, kn_ref, vn_ref, fn_ref, o_ref, qrows_ref, m_ref, l_ref, acc_ref, own_head, ts)


def _attn_sample_new_keys(sq, kn_ref, vn_ref, fn_ref, o_ref, qrows_ref, m_ref, l_ref, acc_ref, own_head, ts):
    nrow = ts * N_HEADS
    qrows = qrows_ref[sq]
    kn = kn_ref[sq].astype(BF16).astype(F32)
    vn = vn_ref[sq].astype(BF16).astype(F32)
    fn = fn_ref[sq]
    eye = (lax.broadcasted_iota(jnp.int32, (N_HEADS, N_HEADS), 0)
           == lax.broadcasted_iota(jnp.int32, (N_HEADS, N_HEADS), 1))
    tok = lax.broadcasted_iota(jnp.int32, (nrow, 1), 0) // N_HEADS
    logits = []
    cn = jnp.zeros((1, N_HEADS), F32)
    for j in range(ts):
        cn = cn + fn[j:j + 1, :]
        cn_col = jnp.sum(jnp.where(eye, jnp.broadcast_to(cn, (N_HEADS, N_HEADS)), 0.0), axis=1, keepdims=True)
        cn_rows = jnp.broadcast_to(cn_col[None, :, :], (ts, N_HEADS, 1)).reshape(nrow, 1)
        sj = jnp.sum(qrows * kn[j:j + 1, :], axis=1, keepdims=True) - cn_rows
        logits.append(jnp.where(tok >= j, sj, NEG_BIG))
    m_prev = m_ref[sq]
    m_fin = m_prev
    for sj in logits:
        m_fin = jnp.maximum(m_fin, sj)
    alpha = jnp.exp(m_prev - m_fin)
    l_fin = alpha * l_ref[sq]
    acc = alpha * acc_ref[sq]
    for j, sj in enumerate(logits):
        pj = jnp.exp(sj - m_fin)
        l_fin = l_fin + pj
        acc = acc + pj * vn[j:j + 1, :]
    out = jnp.where(own_head, acc / l_fin, 0.0)
    o_ref[sq] = jnp.sum(out.reshape(ts, N_HEADS, D_ATT), axis=1)


def _attn_sample(q, kn, vn, fn, logf_pages_t, k_pages_t, v_pages_t, page_table, *, pp=8, ns=2):
    db, ts, _ = q.shape
    n_pages = page_table.shape[1]
    n_groups = n_pages // pp
    nrow = ts * N_HEADS
    smat = jnp.asarray(_suffix_matrix(), dtype=BF16)
    seq = lambda r, d: pl.BlockSpec((ns, r, d), lambda bi, gi, pt: (bi, 0, 0))
    in_hbm = pl.BlockSpec(memory_space=pl.ANY)
    grid_spec = pltpu.PrefetchScalarGridSpec(
        num_scalar_prefetch=1,
        grid=(db // ns, n_groups),
        in_specs=[pl.BlockSpec(smat.shape, lambda bi, gi, pt: (0, 0)),
                  seq(ts, D_ATT), seq(ts, D_ATT), seq(ts, D_ATT), seq(ts, N_HEADS), in_hbm, in_hbm, in_hbm],
        out_specs=seq(ts, D_ATT),
        scratch_shapes=[pltpu.VMEM((2, ns, pp, N_HEADS, PAGE_SIZE), F32),
                        pltpu.VMEM((2, ns, pp, D_ATT, PAGE_SIZE), F32),
                        pltpu.VMEM((2, ns, pp, D_ATT, PAGE_SIZE), F32),
                        pltpu.SemaphoreType.DMA((2, 3)),
                        pltpu.VMEM((ns, nrow, D_ATT), F32), pltpu.VMEM((ns, nrow, 1), F32),
                        pltpu.VMEM((ns, nrow, 1), F32), pltpu.VMEM((ns, nrow, D_ATT), F32),
                        pltpu.VMEM((ns, N_HEADS, PAGE_SIZE), F32)],
    )
    return pl.pallas_call(
        functools.partial(_attn_sample_kernel, pp=pp, ts=ts, ns=ns),
        grid_spec=grid_spec,
        out_shape=jax.ShapeDtypeStruct((db, ts, D_ATT), F32),
        compiler_params=_cparams(("arbitrary", "arbitrary")),
        name="attn_sample",
    )(page_table, smat, q, kn, vn, fn, logf_pages_t, k_pages_t, v_pages_t)


def _out_kernel(att_ref, sga_ref, mixc_ref, x_ref, p_ref, wo_ref, wpe_ref, gpe_ref, wpg_ref, gf_ref, y_ref):
    mix_a = (att_ref[...] * sga_ref[...]).astype(BF16)
    mix_c = mixc_ref[...].astype(BF16)
    h = (x_ref[...] + jnp.dot(mix_a, wo_ref[0:D_ATT, :], preferred_element_type=F32)
         + jnp.dot(mix_c, wo_ref[D_ATT:, :], preferred_element_type=F32))
    pe = jnp.dot(p_ref[...].astype(BF16), wpe_ref[...], preferred_element_type=F32)
    e = pe * lax.rsqrt(jnp.mean(pe * pe, axis=-1, keepdims=True) + EPS) * gpe_ref[...]
    gate = jax.nn.sigmoid(jnp.dot(h.astype(BF16), wpg_ref[...], preferred_element_type=F32))
    h = h + gate * e
    y_ref[...] = h * lax.rsqrt(jnp.mean(h * h, axis=-1, keepdims=True) + EPS) * gf_ref[...]


def _out_stage(att, sga, mixc, x, p, w_out, w_pe, g_pe, w_pg, g_final, *, tm):
    n = x.shape[0]
    tok = lambda d: pl.BlockSpec((tm, d), lambda i: (i, 0))
    const = lambda shape: pl.BlockSpec(shape, lambda i: (0,) * len(shape))
    return pl.pallas_call(
        _out_kernel,
        grid=(n // tm,),
        in_specs=[tok(D_ATT), tok(D_ATT), tok(D_CONV), tok(D_MODEL), tok(D_PLE),
                  const((D_ATT + D_CONV, D_MODEL)), const((D_PLE, D_MODEL)), const((1, D_MODEL)),
                  const((D_MODEL, D_MODEL)), const((1, D_MODEL))],
        out_specs=tok(D_MODEL),
        out_shape=jax.ShapeDtypeStruct((n, D_MODEL), F32),
        compiler_params=_cparams(("arbitrary",)),
        name="out_stage",
    )(att, sga, mixc, x, p, w_out, w_pe, g_pe, w_pg, g_final)


def _in_weights(w_in, b_f):
    a, c, h = D_ATT, D_CONV, N_HEADS
    w = w_in.astype(BF16)
    q, k, v = w[:, 0:a], w[:, a:2 * a], w[:, 2 * a:3 * a]
    f = w[:, 3 * a:3 * a + h]
    o = 3 * a + h
    rest = w[:, o:]
    w_prompt = jnp.concatenate([q, v, rest], axis=1)
    wkt = k.T
    wft = jnp.pad(f.T, ((0, BF16_SUBLANES - h), (0, 0)))
    bf_col = jnp.pad(b_f, (0, BF16_SUBLANES - h)).reshape(BF16_SUBLANES, 1)
    w_sample = jnp.concatenate([q, k, v, rest, jnp.pad(f, ((0, 0), (0, F_PAD - h)))], axis=1)
    bf_row = jnp.pad(b_f, (0, F_PAD - h)).reshape(1, F_PAD)
    return w_prompt, wkt, wft, bf_col, w_sample, bf_row


def kernel(x_prompt, x_sample, cache_k, cache_v, cache_logf, state_conv, page_table, p_prompt, p_sample,
           g_norm, w_in, b_f, w_dw, b_dw, ln_g, ln_b, w_out, w_pe, g_pe, w_pg, g_final):
    assert w_in.shape[0] == 1, "single-layer step"
    b, t, _ = x_prompt.shape
    db, ts, _ = x_sample.shape
    n_phys = cache_k.shape[1]
    row = lambda a: a.reshape(1, -1)

    w_prompt, wkt, wft, bf_col, w_sample, bf_row = _in_weights(w_in[0], b_f[0])
    wo, wpe, wpg = w_out[0].astype(BF16), w_pe[0].astype(BF16), w_pg[0].astype(BF16)
    gn, gpe, gf = row(g_norm[0]), row(g_pe[0]), row(g_final)
    bdw, lng, lnb = row(b_dw[0]), row(ln_g[0]), row(ln_b[0])

    qb, vb, v_p, ktb, kt_p, logft_p, ct, sga_p, mixc_p, u_tail = _proj_prompt(
        x_prompt, gn, w_prompt, wkt, wft, bf_col, w_dw[0], bdw, lng, lnb)
    att_p = _attn_prompt(qb, ktb, vb, ct)
    n_p = b * t
    y_p = _out_stage(att_p.reshape(n_p, D_ATT), sga_p.reshape(n_p, D_ATT), mixc_p.reshape(n_p, D_CONV),
                     x_prompt.reshape(n_p, D_MODEL), p_prompt[0].reshape(n_p, D_PLE),
                     wo, wpe, gpe, wpg, gf, tm=512).reshape(b, t, D_MODEL)
    k_prompt = jnp.transpose(kt_p.reshape(b, N_HEADS, HEAD_DIM, t), (0, 3, 1, 2))[None]
    v_prompt = v_p.reshape(1, b, t, N_HEADS, HEAD_DIM)
    logf_prompt = jnp.transpose(logft_p, (0, 2, 1))[None]
    conv_prompt = u_tail[:, CONV_HALO - CONV_STATE:, :][None]

    n_s = db * ts
    qb_s, k_s, v_s, logf_s, sga_s, u_s, sgc_s = _proj_sample(x_sample.reshape(n_s, D_MODEL), gn, w_sample, bf_row)
    seqs = lambda a: a.reshape(db, ts, a.shape[-1])
    state_t = jnp.transpose(state_conv[0], (1, 0, 2))
    mixc_s, new_state_t = _conv_sample(state_t, seqs(u_s), seqs(sgc_s), w_dw[0], bdw, lng, lnb)
    k_pages_t = jnp.transpose(cache_k[0], (0, 2, 3, 1)).reshape(n_phys, D_ATT, PAGE_SIZE)
    v_pages_t = jnp.transpose(cache_v[0], (0, 2, 3, 1)).reshape(n_phys, D_ATT, PAGE_SIZE)
    logf_pages_t = jnp.transpose(cache_logf[0], (0, 2, 1))
    att_s = _attn_sample(seqs(qb_s), seqs(k_s), seqs(v_s), seqs(logf_s), logf_pages_t, k_pages_t, v_pages_t,
                         page_table)
    y_s = _out_stage(att_s.reshape(n_s, D_ATT), sga_s, mixc_s.reshape(n_s, D_CONV),
                     x_sample.reshape(n_s, D_MODEL), p_sample[0].reshape(n_s, D_PLE),
                     wo, wpe, gpe, wpg, gf, tm=n_s).reshape(db, ts, D_MODEL)
    conv_sample = jnp.transpose(new_state_t, (1, 0, 2))[None]

    heads = lambda a: a.reshape(1, db, ts, N_HEADS, HEAD_DIM)
    return (y_p, y_s, k_prompt, v_prompt, logf_prompt, conv_prompt,
            heads(k_s), heads(v_s), logf_s.reshape(1, db, ts, N_HEADS), conv_sample)
```

```python
import functools

import numpy as np
import jax
import jax.numpy as jnp
from jax import lax
from jax.experimental import pallas as pl
from jax.experimental.pallas import tpu as pltpu

D_MODEL = 1024
D_ATT = 512
D_CONV = 512
N_HEADS = 8
HEAD_DIM = 64
CONV_WIDTH = 31
CONV_STATE = CONV_WIDTH - 1
D_PLE = 256
PAGE_SIZE = 128
EPS = 1e-6
ATT_SCALE = HEAD_DIM ** -0.5
LOG2E = 1.4426950408889634
Q_PRESCALE = ATT_SCALE * LOG2E
NEG_BIG = -1e30

LANES = 128
SUBLANES = 8
BF16_SUBLANES = 16
F_PAD = LANES

VMEM_LIMIT = 56 * 1024 * 1024

F32 = jnp.float32
BF16 = jnp.bfloat16
NT_DIMS = (((1,), (1,)), ((), ()))


def _silu(x):
    return x * jax.nn.sigmoid(x)


def _cparams(sem):
    return pltpu.CompilerParams(dimension_semantics=sem, vmem_limit_bytes=VMEM_LIMIT)


def _rms_normed(x_ref, g_ref):
    x = x_ref[...]
    ms = jnp.mean(x * x, axis=-1, keepdims=True)
    return ((x * lax.rsqrt(ms + EPS)) * g_ref[...]).astype(BF16)


PW_Q, PW_V, PW_GA, PW_GLUA, PW_GLUB, PW_GC = 0, 512, 1024, 1536, 2048, 2560
PW_COLS = 3072


CONV_HALO = 32


def _ln_swish(y, lg_ref, lb_ref):
    mu = jnp.mean(y, axis=-1, keepdims=True)
    yc = y - mu
    var = jnp.mean(yc * yc, axis=-1, keepdims=True)
    return _silu(yc * lax.rsqrt(var + EPS) * lg_ref[...] + lb_ref[...])


def _proj_prompt_kernel(x_ref, g_ref, w_ref, wkt_ref, wft_ref, bf_ref, wdw_ref, bdw_ref, lg_ref, lb_ref,
                        qb_ref, vb_ref, vf_ref, ktb_ref, ktf_ref, logft_ref, ct_ref, sga_ref, mixc_ref, tail_ref,
                        carry_ref, ext_ref, shift_ref, gate_ref, *, tm, rc):
    xn = _rms_normed(x_ref, g_ref)

    def proj(lo, width):
        return jnp.dot(xn, w_ref[:, lo:lo + width], preferred_element_type=F32)

    @pl.when(pl.program_id(1) == 0)
    def _():
        carry_ref[...] = jnp.zeros_like(carry_ref)
        ext_ref[0:CONV_HALO, :] = jnp.zeros((CONV_HALO, D_CONV), F32)

    ext_ref[CONV_HALO:CONV_HALO + tm, :] = proj(PW_GLUA, D_CONV) * jax.nn.sigmoid(proj(PW_GLUB, D_CONV))
    gate_ref[...] = _silu(proj(PW_GC, D_CONV))
    tail_ref[...] = ext_ref[tm:tm + CONV_HALO, :]

    def section_q():
        qb_ref[...] = (proj(PW_Q, D_ATT) * Q_PRESCALE).astype(BF16)

    def section_v():
        v = proj(PW_V, D_ATT)
        vf_ref[...] = v
        vb_ref[...] = v.astype(BF16)

    def section_ga():
        sga_ref[...] = _silu(proj(PW_GA, D_ATT))

    def section_k():
        kt = lax.dot_general(wkt_ref[...], xn, NT_DIMS, preferred_element_type=F32)
        ktf_ref[...] = kt
        ktb_ref[...] = kt.astype(BF16)

    def section_f():
        fzt = lax.dot_general(wft_ref[...], xn, NT_DIMS, preferred_element_type=F32)
        logft = jax.nn.log_sigmoid(fzt + bf_ref[...])[:N_HEADS, :]
        logft_ref[...] = logft
        lane = lax.broadcasted_iota(jnp.int32, (N_HEADS, tm), 1)
        c = logft
        shift = 1
        while shift < tm:
            c = c + jnp.where(lane >= shift, pltpu.roll(c, shift, axis=1), 0.0)
            shift *= 2
        c = c + carry_ref[:, 0:1]
        ct_ref[...] = c
        carry_ref[...] = jnp.broadcast_to(c[:, tm - 1:tm], carry_ref.shape)

    off = CONV_HALO - CONV_STATE
    span = rc + CONV_HALO - SUBLANES

    def conv_chunk(ci):
        r0 = ci * rc
        slot = ci % 2
        win = ext_ref[r0:r0 + rc + CONV_HALO, :]
        for r in range(1, SUBLANES):
            shift_ref[slot, r - 1] = win[r:r + span, :]
        acc = jnp.zeros((rc, D_CONV), F32)
        for j in range(CONV_WIDTH):
            r = (off + j) % SUBLANES
            a = off + j - r
            tap = win[a:a + rc, :] if r == 0 else shift_ref[slot, r - 1, a:a + rc, :]
            acc = acc + tap * wdw_ref[j:j + 1, :]
        mixc_ref[r0:r0 + rc, :] = _ln_swish(acc + bdw_ref[...], lg_ref, lb_ref) * gate_ref[r0:r0 + rc, :]

    sections = [section_q, section_v, section_ga, section_k, section_f]
    nchunk = tm // rc
    bounds = [nchunk * n // len(sections) for n in range(len(sections) + 1)]
    for n, section in enumerate(sections):
        section()
        for ci in range(bounds[n], bounds[n + 1]):
            conv_chunk(ci)
    ext_ref[0:CONV_HALO, :] = ext_ref[tm:tm + CONV_HALO, :]


def _proj_prompt(x, g_norm, w_tok, wkt, wft, bf_col, w_dw, b_dw, ln_g, ln_b, *, tm=512, rc=32):
    b, t, _ = x.shape
    tok = lambda d: pl.BlockSpec((None, tm, d), lambda i, j: (i, j, 0))
    tr = lambda r: pl.BlockSpec((None, r, tm), lambda i, j: (i, 0, j))
    const = lambda shape: pl.BlockSpec(shape, lambda i, j: (0,) * len(shape))
    out_shape = (
        jax.ShapeDtypeStruct((b, t, D_ATT), BF16),
        jax.ShapeDtypeStruct((b, t, D_ATT), BF16),
        jax.ShapeDtypeStruct((b, t, D_ATT), F32),
        jax.ShapeDtypeStruct((b, D_ATT, t), BF16),
        jax.ShapeDtypeStruct((b, D_ATT, t), F32),
        jax.ShapeDtypeStruct((b, N_HEADS, t), F32),
        jax.ShapeDtypeStruct((b, N_HEADS, t), F32),
        jax.ShapeDtypeStruct((b, t, D_ATT), F32),
        jax.ShapeDtypeStruct((b, t, D_CONV), F32),
        jax.ShapeDtypeStruct((b, CONV_HALO, D_CONV), F32),
    )
    out_specs = (tok(D_ATT), tok(D_ATT), tok(D_ATT), tr(D_ATT), tr(D_ATT), tr(N_HEADS), tr(N_HEADS),
                 tok(D_ATT), tok(D_CONV), pl.BlockSpec((None, CONV_HALO, D_CONV), lambda i, j: (i, 0, 0)))
    return pl.pallas_call(
        functools.partial(_proj_prompt_kernel, tm=tm, rc=rc),
        grid=(b, t // tm),
        in_specs=[tok(D_MODEL), const((1, D_MODEL)), const(w_tok.shape), const(wkt.shape), const(wft.shape),
                  const(bf_col.shape), const((CONV_WIDTH, D_CONV)), const((1, D_CONV)), const((1, D_CONV)),
                  const((1, D_CONV))],
        out_specs=out_specs,
        out_shape=out_shape,
        scratch_shapes=[pltpu.VMEM((N_HEADS, LANES), F32),
                        pltpu.VMEM((tm + CONV_HALO, D_CONV), F32),
                        pltpu.VMEM((2, SUBLANES - 1, rc + CONV_HALO - SUBLANES, D_CONV), F32),
                        pltpu.VMEM((tm, D_CONV), F32)],
        compiler_params=_cparams(("arbitrary", "arbitrary")),
        name="proj_prompt",
    )(x, g_norm, w_tok, wkt, wft, bf_col, w_dw, b_dw, ln_g, ln_b)


SW_Q, SW_K, SW_V, SW_GA, SW_GLUA, SW_GLUB, SW_GC, SW_F = 0, 512, 1024, 1536, 2048, 2560, 3072, 3584
SW_COLS = SW_F + F_PAD


def _proj_sample_kernel(x_ref, g_ref, w_ref, bf_ref, qb_ref, k_ref, v_ref, logf_ref, sga_ref, u_ref, sgc_ref):
    xn = _rms_normed(x_ref, g_ref)

    def proj(lo, width):
        return jnp.dot(xn, w_ref[:, lo:lo + width], preferred_element_type=F32)

    qb_ref[...] = (proj(SW_Q, D_ATT) * ATT_SCALE).astype(BF16)
    k_ref[...] = proj(SW_K, D_ATT)
    v_ref[...] = proj(SW_V, D_ATT)
    sga_ref[...] = _silu(proj(SW_GA, D_ATT))
    u_ref[...] = proj(SW_GLUA, D_CONV) * jax.nn.sigmoid(proj(SW_GLUB, D_CONV))
    sgc_ref[...] = _silu(proj(SW_GC, D_CONV))
    logf_ref[...] = jax.nn.log_sigmoid(proj(SW_F, F_PAD) + bf_ref[...])[:, :N_HEADS]


def _proj_sample(x, g_norm, w_tok, bf_row):
    n = x.shape[0]
    full = lambda shape: pl.BlockSpec(shape, lambda i: (0,) * len(shape))
    wide = jax.ShapeDtypeStruct((n, D_ATT), F32)
    out_shape = (jax.ShapeDtypeStruct((n, D_ATT), BF16), wide, wide, jax.ShapeDtypeStruct((n, N_HEADS), F32),
                 wide, wide, wide)
    return pl.pallas_call(
        _proj_sample_kernel,
        grid=(1,),
        in_specs=[full(x.shape), full(g_norm.shape), full(w_tok.shape), full(bf_row.shape)],
        out_specs=tuple(full(s.shape) for s in out_shape),
        out_shape=out_shape,
        compiler_params=_cparams(("arbitrary",)),
        name="proj_sample",
    )(x, g_norm, w_tok, bf_row)


def _conv_sample_kernel(st_ref, u_ref, sgc_ref, w_ref, b_ref, lg_ref, lb_ref, o_ref, ns_ref, *, ts):
    def ext(i):
        return st_ref[i] if i < CONV_STATE else u_ref[:, i - CONV_STATE, :]

    for t in range(ts):
        acc = jnp.zeros((o_ref.shape[0], D_CONV), F32)
        for j in range(CONV_WIDTH):
            acc = acc + ext(t + j) * w_ref[j:j + 1, :]
        o_ref[:, t, :] = _ln_swish(acc + b_ref[...], lg_ref, lb_ref) * sgc_ref[:, t, :]
    for i in range(CONV_STATE):
        ns_ref[i] = ext(i + ts)


def _conv_sample(state_t, u, sgc, w_dw, b_dw, ln_g, ln_b, *, bs=32):
    db, ts, _ = u.shape
    seq = pl.BlockSpec((bs, ts, D_CONV), lambda i: (i, 0, 0))
    st = pl.BlockSpec((CONV_STATE, bs, D_CONV), lambda i: (0, i, 0))
    const = lambda shape: pl.BlockSpec(shape, lambda i: (0,) * len(shape))
    return pl.pallas_call(
        functools.partial(_conv_sample_kernel, ts=ts),
        grid=(db // bs,),
        in_specs=[st, seq, seq, const((CONV_WIDTH, D_CONV)), const((1, D_CONV)), const((1, D_CONV)),
                  const((1, D_CONV))],
        out_specs=(seq, st),
        out_shape=(jax.ShapeDtypeStruct((db, ts, D_CONV), F32),
                   jax.ShapeDtypeStruct((CONV_STATE, db, D_CONV), F32)),
        compiler_params=_cparams(("arbitrary",)),
        name="conv_sample",
    )(state_t, u, sgc, w_dw, b_dw, ln_g, ln_b)


def _attn_prompt_kernel(q_ref, kt_ref, v_ref, ct_ref, o_ref, s0_ref, mblk0_ref, alpha0_ref,
                        s1_ref, mblk1_ref, alpha1_ref, m_ref, acc_ref, *, tq, rq):
    pair = pl.program_id(1)
    i = pl.program_id(2)
    q0 = pl.multiple_of(i * tq, tq)
    q2 = q_ref[...]
    lane = lax.broadcasted_iota(jnp.int32, (tq, LANES), 1)
    in_head = (lane < HEAD_DIM, lane >= HEAD_DIM)
    qh = [jnp.where(in_head[hh], q2, jnp.zeros_like(q2)) for hh in range(2)]
    head_row = lax.broadcasted_iota(jnp.int32, (N_HEADS, tq), 0)
    chains = [(hh, r0) for hh in range(2) for r0 in range(0, tq, rq)]

    def c_rows(k0):
        blk = ct_ref[:, pl.ds(k0, tq)] * LOG2E
        return [jnp.sum(jnp.where(head_row == 2 * pair + hh, blk, 0.0), axis=0, keepdims=True) for hh in range(2)]

    c_first = [r[:, 0:1] for r in c_rows(q0)]
    m_ref[...] = jnp.full_like(m_ref, NEG_BIG)
    acc_ref[...] = jnp.zeros_like(acc_ref)

    slots = ((s0_ref, mblk0_ref, alpha0_ref), (s1_ref, mblk1_ref, alpha1_ref))

    def scores(c, slot, masked, ktb, bias):
        s_ref, mblk_ref, alpha_ref = slots[slot]
        hh, r0 = chains[c]
        s = jnp.dot(qh[hh][r0:r0 + rq, :], ktb, preferred_element_type=F32) + bias[hh]
        if masked:
            row = lax.broadcasted_iota(jnp.int32, (rq, tq), 0) + r0
            col = lax.broadcasted_iota(jnp.int32, (rq, tq), 1)
            s = jnp.where(row >= col, s, NEG_BIG)
        m_prev = m_ref[c]
        m_new = jnp.maximum(m_prev, jnp.max(s, axis=1, keepdims=True))
        s_ref[c] = s
        mblk_ref[c] = m_new
        alpha_ref[c] = jnp.exp2(m_prev - m_new)
        m_ref[c] = m_new

    def accumulate(c, slot, vh):
        s_ref, mblk_ref, alpha_ref = slots[slot]
        hh, _ = chains[c]
        m_blk = mblk_ref[c]
        p = jnp.concatenate([jnp.exp2(s_ref[c, :, k * LANES:(k + 1) * LANES] - m_blk)
                             for k in range(tq // LANES)], axis=1)
        acc_ref[c] = alpha_ref[c] * acc_ref[c] + jnp.dot(p.astype(BF16), vh[hh], preferred_element_type=F32)

    def k_side(j):
        k0 = pl.multiple_of(j * tq, tq)
        c_keys = c_rows(k0)
        return kt_ref[:, pl.ds(k0, tq)], [c_first[hh] - c_keys[hh] for hh in range(2)]

    def v_side(j):
        vb = v_ref[pl.ds(pl.multiple_of(j * tq, tq), tq), :]
        return [jnp.where(in_head[hh], vb, jnp.ones_like(vb)) for hh in range(2)]

    def step(j, wslot, jprev):
        ktb, bias = k_side(j)
        vh = v_side(jprev)
        for c in range(len(chains)):
            scores(c, wslot, False, ktb, bias)
            accumulate(c, 1 - wslot, vh)

    def drain(slot, jlast):
        vh = v_side(jlast)
        for c in range(len(chains)):
            accumulate(c, slot, vh)

    ktb, bias = k_side(i)
    for c in range(len(chains)):
        scores(c, 0, True, ktb, bias)

    def two_steps(jj, carry):
        j = 2 * jj
        step(j, 1, jnp.where(jj == 0, i, j - 1))
        step(j + 1, 0, j)
        return carry

    lax.fori_loop(0, i // 2, two_steps, 0)

    @pl.when(i % 2 == 1)
    def _():
        step(i - 1, 1, jnp.where(i == 1, i, i - 2))
        drain(1, i - 1)

    @pl.when(i % 2 == 0)
    def _():
        drain(0, jnp.where(i == 0, i, i - 1))

    def head_out(hh):
        acc = jnp.concatenate([acc_ref[c] for c in range(len(chains)) if chains[c][0] == hh], axis=0)
        return acc / pltpu.roll(acc, HEAD_DIM, axis=1)

    o_ref[...] = jnp.where(in_head[0], head_out(0), head_out(1))


def _attn_prompt(qb, ktb, vb, ct, *, tq=1024, rq=256):
    b, t, _ = qb.shape
    nchain = 2 * (tq // rq)
    return pl.pallas_call(
        functools.partial(_attn_prompt_kernel, tq=tq, rq=rq),
        grid=(b, N_HEADS // 2, t // tq),
        in_specs=[pl.BlockSpec((None, tq, LANES), lambda bi, pi, qi: (bi, qi, pi)),
                  pl.BlockSpec((None, LANES, t), lambda bi, pi, qi: (bi, pi, 0)),
                  pl.BlockSpec((None, t, LANES), lambda bi, pi, qi: (bi, 0, pi)),
                  pl.BlockSpec((None, N_HEADS, t), lambda bi, pi, qi: (bi, 0, 0))],
        out_specs=pl.BlockSpec((None, tq, LANES), lambda bi, pi, qi: (bi, qi, pi)),
        out_shape=jax.ShapeDtypeStruct((b, t, D_ATT), F32),
        scratch_shapes=[pltpu.VMEM((nchain, rq, tq), F32),
                        pltpu.VMEM((nchain, rq, LANES), F32),
                        pltpu.VMEM((nchain, rq, LANES), F32)] * 2
                       + [pltpu.VMEM((nchain, rq, LANES), F32),
                          pltpu.VMEM((nchain, rq, LANES), F32)],
        compiler_params=_cparams(("arbitrary", "arbitrary", "arbitrary")),
        name="attn_prompt",
    )(qb, ktb, vb, ct)


def _suffix_matrix():
    after = (np.arange(PAGE_SIZE)[:, None] > np.arange(PAGE_SIZE)[None, :]).astype(np.float32)
    return np.concatenate([after, np.ones_like(after)], axis=1)


def _attn_sample_kernel(pt_ref, smat_ref, q_ref, kn_ref, vn_ref, fn_ref, f_hbm, kt_hbm, vt_hbm, o_ref,
                        fbuf, ktbuf, vtbuf, sems, qrows_ref, m_ref, l_ref, acc_ref, after_ref, *, pp, ts, ns):
    bi = pl.program_id(0)
    g = pl.program_id(1)
    n_groups = pl.num_programs(1)
    step = bi * n_groups + g
    cur = step % 2
    nrow = ts * N_HEADS

    def page_copies(b, grp, buf):
        copies = []
        for sq in range(ns):
            for p in range(pp):
                page = pt_ref[b * ns + sq, (n_groups - 1 - grp) * pp + p]
                for kind, (hbm, vmem) in enumerate(((f_hbm, fbuf), (kt_hbm, ktbuf), (vt_hbm, vtbuf))):
                    copies.append(pltpu.make_async_copy(hbm.at[page], vmem.at[buf, sq, p], sems.at[buf, kind]))
        return copies

    @pl.when(step == 0)
    def _():
        for cp in page_copies(bi, g, cur):
            cp.start()

    @pl.when(step + 1 < pl.num_programs(0) * n_groups)
    def _():
        wrap = g + 1 == n_groups
        for cp in page_copies(jnp.where(wrap, bi + 1, bi), jnp.where(wrap, 0, g + 1), 1 - cur):
            cp.start()

    for cp in page_copies(bi, g, cur):
        cp.wait()
    row = lax.broadcasted_iota(jnp.int32, (nrow, D_ATT), 0)
    lane = lax.broadcasted_iota(jnp.int32, (nrow, D_ATT), 1)
    own_head = (lane // HEAD_DIM) == (row % N_HEADS)

    @pl.when(g == 0)
    def _():
        for sq in range(ns):
            q = q_ref[sq].astype(F32)
            qb = jnp.broadcast_to(q[:, None, :], (ts, N_HEADS, D_ATT)).reshape(nrow, D_ATT)
            qrows_ref[sq] = jnp.where(own_head, qb, 0.0)
        m_ref[...] = jnp.full_like(m_ref, NEG_BIG)
        l_ref[...] = jnp.zeros_like(l_ref)
        acc_ref[...] = jnp.zeros_like(acc_ref)
        after_ref[...] = jnp.zeros_like(after_ref)

    smat = smat_ref[...]
    for sq in range(ns):
        lhs = fbuf[cur, sq].reshape(pp * N_HEADS, PAGE_SIZE)
        hi = lhs.astype(BF16)
        r1 = lhs - hi.astype(F32)
        mid = r1.astype(BF16)
        lo = (r1 - mid.astype(F32)).astype(BF16)
        sums = (jnp.dot(hi, smat, preferred_element_type=F32) + jnp.dot(mid, smat, preferred_element_type=F32)
                + jnp.dot(lo, smat, preferred_element_type=F32))
        after = after_ref[sq]
        bias = [None] * pp
        for p in reversed(range(pp)):
            blk = sums[p * N_HEADS:(p + 1) * N_HEADS, :]
            bias[p] = blk[:, :PAGE_SIZE] + after
            after = after + blk[:, PAGE_SIZE:]
        after_ref[sq] = after

        qrows = qrows_ref[sq]
        scores = []
        for p in range(pp):
            s = jnp.dot(qrows, ktbuf[cur, sq, p], preferred_element_type=F32)
            s = s.reshape(ts, N_HEADS, PAGE_SIZE) + bias[p][None, :, :]
            scores.append(s.reshape(nrow, PAGE_SIZE))
        s_all = jnp.concatenate(scores, axis=1)
        m_prev = m_ref[sq]
        m_new = jnp.maximum(m_prev, jnp.max(s_all, axis=1, keepdims=True))
        alpha = jnp.exp(m_prev - m_new)
        prob = jnp.exp(s_all - m_new)
        l_ref[sq] = alpha * l_ref[sq] + jnp.sum(prob, axis=1, keepdims=True)
        pv = jnp.zeros((nrow, D_ATT), F32)
        for p in range(pp):
            pv = pv + lax.dot_general(prob[:, p * PAGE_SIZE:(p + 1) * PAGE_SIZE], vtbuf[cur, sq, p], NT_DIMS,
                                      preferred_element_type=F32)
        acc_ref[sq] = alpha * acc_ref[sq] + pv
        m_ref[sq] = m_new

    @pl.when(g == pl.num_programs(1) - 1)
    def _():
        for sq in range(ns):
            _attn_sample_new_keys(sq, kn_ref, vn_ref, fn_ref, o_ref, qrows_ref, m_ref, l_ref, acc_ref, own_head, ts)


def _attn_sample_new_keys(sq, kn_ref, vn_ref, fn_ref, o_ref, qrows_ref, m_ref, l_ref, acc_ref, own_head, ts):
    nrow = ts * N_HEADS
    qrows = qrows_ref[sq]
    kn = kn_ref[sq].astype(BF16).astype(F32)
    vn = vn_ref[sq].astype(BF16).astype(F32)
    fn = fn_ref[sq]
    eye = (lax.broadcasted_iota(jnp.int32, (N_HEADS, N_HEADS), 0)
           == lax.broadcasted_iota(jnp.int32, (N_HEADS, N_HEADS), 1))
    tok = lax.broadcasted_iota(jnp.int32, (nrow, 1), 0) // N_HEADS
    logits = []
    cn = jnp.zeros((1, N_HEADS), F32)
    for j in range(ts):
        cn = cn + fn[j:j + 1, :]
        cn_col = jnp.sum(jnp.where(eye, jnp.broadcast_to(cn, (N_HEADS, N_HEADS)), 0.0), axis=1, keepdims=True)
        cn_rows = jnp.broadcast_to(cn_col[None, :, :], (ts, N_HEADS, 1)).reshape(nrow, 1)
        sj = jnp.sum(qrows * kn[j:j + 1, :], axis=1, keepdims=True) - cn_rows
        logits.append(jnp.where(tok >= j, sj, NEG_BIG))
    m_prev = m_ref[sq]
    m_fin = m_prev
    for sj in logits:
        m_fin = jnp.maximum(m_fin, sj)
    alpha = jnp.exp(m_prev - m_fin)
    l_fin = alpha * l_ref[sq]
    acc = alpha * acc_ref[sq]
    for j, sj in enumerate(logits):
        pj = jnp.exp(sj - m_fin)
        l_fin = l_fin + pj
        acc = acc + pj * vn[j:j + 1, :]
    out = jnp.where(own_head, acc / l_fin, 0.0)
    o_ref[sq] = jnp.sum(out.reshape(ts, N_HEADS, D_ATT), axis=1)


def _attn_sample(q, kn, vn, fn, logf_pages_t, k_pages_t, v_pages_t, page_table, *, pp=8, ns=2):
    db, ts, _ = q.shape
    n_pages = page_table.shape[1]
    n_groups = n_pages // pp
    nrow = ts * N_HEADS
    smat = jnp.asarray(_suffix_matrix(), dtype=BF16)
    seq = lambda r, d: pl.BlockSpec((ns, r, d), lambda bi, gi, pt: (bi, 0, 0))
    in_hbm = pl.BlockSpec(memory_space=pl.ANY)
    grid_spec = pltpu.PrefetchScalarGridSpec(
        num_scalar_prefetch=1,
        grid=(db // ns, n_groups),
        in_specs=[pl.BlockSpec(smat.shape, lambda bi, gi, pt: (0, 0)),
                  seq(ts, D_ATT), seq(ts, D_ATT), seq(ts, D_ATT), seq(ts, N_HEADS), in_hbm, in_hbm, in_hbm],
        out_specs=seq(ts, D_ATT),
        scratch_shapes=[pltpu.VMEM((2, ns, pp, N_HEADS, PAGE_SIZE), F32),
                        pltpu.VMEM((2, ns, pp, D_ATT, PAGE_SIZE), F32),
                        pltpu.VMEM((2, ns, pp, D_ATT, PAGE_SIZE), F32),
                        pltpu.SemaphoreType.DMA((2, 3)),
                        pltpu.VMEM((ns, nrow, D_ATT), F32), pltpu.VMEM((ns, nrow, 1), F32),
                        pltpu.VMEM((ns, nrow, 1), F32), pltpu.VMEM((ns, nrow, D_ATT), F32),
                        pltpu.VMEM((ns, N_HEADS, PAGE_SIZE), F32)],
    )
    return pl.pallas_call(
        functools.partial(_attn_sample_kernel, pp=pp, ts=ts, ns=ns),
        grid_spec=grid_spec,
        out_shape=jax.ShapeDtypeStruct((db, ts, D_ATT), F32),
        compiler_params=_cparams(("arbitrary", "arbitrary")),
        name="attn_sample",
    )(page_table, smat, q, kn, vn, fn, logf_pages_t, k_pages_t, v_pages_t)


def _out_kernel(att_ref, sga_ref, mixc_ref, x_ref, p_ref, wo_ref, wpe_ref, gpe_ref, wpg_ref, gf_ref, y_ref):
    mix_a = (att_ref[...] * sga_ref[...]).astype(BF16)
    mix_c = mixc_ref[...].astype(BF16)
    h = (x_ref[...] + jnp.dot(mix_a, wo_ref[0:D_ATT, :], preferred_element_type=F32)
         + jnp.dot(mix_c, wo_ref[D_ATT:, :], preferred_element_type=F32))
    pe = jnp.dot(p_ref[...].astype(BF16), wpe_ref[...], preferred_element_type=F32)
    e = pe * lax.rsqrt(jnp.mean(pe * pe, axis=-1, keepdims=True) + EPS) * gpe_ref[...]
    gate = jax.nn.sigmoid(jnp.dot(h.astype(BF16), wpg_ref[...], preferred_element_type=F32))
    h = h + gate * e
    y_ref[...] = h * lax.rsqrt(jnp.mean(h * h, axis=-1, keepdims=True) + EPS) * gf_ref[...]


def _out_stage(att, sga, mixc, x, p, w_out, w_pe, g_pe, w_pg, g_final, *, tm):
    n = x.shape[0]
    tok = lambda d: pl.BlockSpec((tm, d), lambda i: (i, 0))
    const = lambda shape: pl.BlockSpec(shape, lambda i: (0,) * len(shape))
    return pl.pallas_call(
        _out_kernel,
        grid=(n // tm,),
        in_specs=[tok(D_ATT), tok(D_ATT), tok(D_CONV), tok(D_MODEL), tok(D_PLE),
                  const((D_ATT + D_CONV, D_MODEL)), const((D_PLE, D_MODEL)), const((1, D_MODEL)),
                  const((D_MODEL, D_MODEL)), const((1, D_MODEL))],
        out_specs=tok(D_MODEL),
        out_shape=jax.ShapeDtypeStruct((n, D_MODEL), F32),
        compiler_params=_cparams(("arbitrary",)),
        name="out_stage",
    )(att, sga, mixc, x, p, w_out, w_pe, g_pe, w_pg, g_final)


def _in_weights(w_in, b_f):
    a, c, h = D_ATT, D_CONV, N_HEADS
    w = w_in.astype(BF16)
    q, k, v = w[:, 0:a], w[:, a:2 * a], w[:, 2 * a:3 * a]
    f = w[:, 3 * a:3 * a + h]
    o = 3 * a + h
    rest = w[:, o:]
    w_prompt = jnp.concatenate([q, v, rest], axis=1)
    wkt = k.T
    wft = jnp.pad(f.T, ((0, BF16_SUBLANES - h), (0, 0)))
    bf_col = jnp.pad(b_f, (0, BF16_SUBLANES - h)).reshape(BF16_SUBLANES, 1)
    w_sample = jnp.concatenate([q, k, v, rest, jnp.pad(f, ((0, 0), (0, F_PAD - h)))], axis=1)
    bf_row = jnp.pad(b_f, (0, F_PAD - h)).reshape(1, F_PAD)
    return w_prompt, wkt, wft, bf_col, w_sample, bf_row


def kernel(x_prompt, x_sample, cache_k, cache_v, cache_logf, state_conv, page_table, p_prompt, p_sample,
           g_norm, w_in, b_f, w_dw, b_dw, ln_g, ln_b, w_out, w_pe, g_pe, w_pg, g_final):
    assert w_in.shape[0] == 1, "single-layer step"
    b, t, _ = x_prompt.shape
    db, ts, _ = x_sample.shape
    n_phys = cache_k.shape[1]
    row = lambda a: a.reshape(1, -1)

    w_prompt, wkt, wft, bf_col, w_sample, bf_row = _in_weights(w_in[0], b_f[0])
    wo, wpe, wpg = w_out[0].astype(BF16), w_pe[0].astype(BF16), w_pg[0].astype(BF16)
    gn, gpe, gf = row(g_norm[0]), row(g_pe[0]), row(g_final)
    bdw, lng, lnb = row(b_dw[0]), row(ln_g[0]), row(ln_b[0])

    qb, vb, v_p, ktb, kt_p, logft_p, ct, sga_p, mixc_p, u_tail = _proj_prompt(
        x_prompt, gn, w_prompt, wkt, wft, bf_col, w_dw[0], bdw, lng, lnb)
    att_p = _attn_prompt(qb, ktb, vb, ct)
    n_p = b * t
    y_p = _out_stage(att_p.reshape(n_p, D_ATT), sga_p.reshape(n_p, D_ATT), mixc_p.reshape(n_p, D_CONV),
                     x_prompt.reshape(n_p, D_MODEL), p_prompt[0].reshape(n_p, D_PLE),
                     wo, wpe, gpe, wpg, gf, tm=512).reshape(b, t, D_MODEL)
    k_prompt = jnp.transpose(kt_p.reshape(b, N_HEADS, HEAD_DIM, t), (0, 3, 1, 2))[None]
    v_prompt = v_p.reshape(1, b, t, N_HEADS, HEAD_DIM)
    logf_prompt = jnp.transpose(logft_p, (0, 2, 1))[None]
    conv_prompt = u_tail[:, CONV_HALO - CONV_STATE:, :][None]

    n_s = db * ts
    qb_s, k_s, v_s, logf_s, sga_s, u_s, sgc_s = _proj_sample(x_sample.reshape(n_s, D_MODEL), gn, w_sample, bf_row)
    seqs = lambda a: a.reshape(db, ts, a.shape[-1])
    state_t = jnp.transpose(state_conv[0], (1, 0, 2))
    mixc_s, new_state_t = _conv_sample(state_t, seqs(u_s), seqs(sgc_s), w_dw[0], bdw, lng, lnb)
    k_pages_t = jnp.transpose(cache_k[0], (0, 2, 3, 1)).reshape(n_phys, D_ATT, PAGE_SIZE)
    v_pages_t = jnp.transpose(cache_v[0], (0, 2, 3, 1)).reshape(n_phys, D_ATT, PAGE_SIZE)
    logf_pages_t = jnp.transpose(cache_logf[0], (0, 2, 1))
    att_s = _attn_sample(seqs(qb_s), seqs(k_s), seqs(v_s), seqs(logf_s), logf_pages_t, k_pages_t, v_pages_t,
                         page_table)
    y_s = _out_stage(att_s.reshape(n_s, D_ATT), sga_s, mixc_s.reshape(n_s, D_CONV),
                     x_sample.reshape(n_s, D_MODEL), p_sample[0].reshape(n_s, D_PLE),
                     wo, wpe, gpe, wpg, gf, tm=n_s).reshape(db, ts, D_MODEL)
    conv_sample = jnp.transpose(new_state_t, (1, 0, 2))[None]

    heads = lambda a: a.reshape(1, db, ts, N_HEADS, HEAD_DIM)
    return (y_p, y_s, k_prompt, v_prompt, logf_prompt, conv_prompt,
            heads(k_s), heads(v_s), logf_s.reshape(1, db, ts, N_HEADS), conv_sample)
```

```python
import functools

import numpy as np
import jax
import jax.numpy as jnp
from jax import lax
from jax.experimental import pallas as pl
from jax.experimental.pallas import tpu as pltpu

D_MODEL = 1024
D_ATT = 512
D_CONV = 512
N_HEADS = 8
HEAD_DIM = 64
CONV_WIDTH = 31
CONV_STATE = CONV_WIDTH - 1
D_PLE = 256
PAGE_SIZE = 128
EPS = 1e-6
ATT_SCALE = HEAD_DIM ** -0.5
LOG2E = 1.4426950408889634
Q_PRESCALE = ATT_SCALE * LOG2E
NEG_BIG = -1e30

LANES = 128
SUBLANES = 8
BF16_SUBLANES = 16
F_PAD = LANES

VMEM_LIMIT = 56 * 1024 * 1024

F32 = jnp.float32
BF16 = jnp.bfloat16
NT_DIMS = (((1,), (1,)), ((), ()))


def _silu(x):
    return x * jax.nn.sigmoid(x)


def _cparams(sem):
    return pltpu.CompilerParams(dimension_semantics=sem, vmem_limit_bytes=VMEM_LIMIT)


def _rms_normed(x_ref, g_ref):
    x = x_ref[...]
    ms = jnp.mean(x * x, axis=-1, keepdims=True)
    return ((x * lax.rsqrt(ms + EPS)) * g_ref[...]).astype(BF16)


PW_Q, PW_V, PW_GA, PW_GLUA, PW_GLUB, PW_GC = 0, 512, 1024, 1536, 2048, 2560
PW_COLS = 3072


CONV_HALO = 32


def _ln_swish(y, lg_ref, lb_ref):
    mu = jnp.mean(y, axis=-1, keepdims=True)
    yc = y - mu
    var = jnp.mean(yc * yc, axis=-1, keepdims=True)
    return _silu(yc * lax.rsqrt(var + EPS) * lg_ref[...] + lb_ref[...])


def _proj_prompt_kernel(x_ref, g_ref, w_ref, wkt_ref, wft_ref, bf_ref, wdw_ref, bdw_ref, lg_ref, lb_ref,
                        qb_ref, vb_ref, vf_ref, ktb_ref, ktf_ref, logft_ref, ct_ref, sga_ref, mixc_ref, tail_ref,
                        carry_ref, ext_ref, shift_ref, gate_ref, *, tm, rc):
    xn = _rms_normed(x_ref, g_ref)

    def proj(lo, width):
        return jnp.dot(xn, w_ref[:, lo:lo + width], preferred_element_type=F32)

    @pl.when(pl.program_id(1) == 0)
    def _():
        carry_ref[...] = jnp.zeros_like(carry_ref)
        ext_ref[0:CONV_HALO, :] = jnp.zeros((CONV_HALO, D_CONV), F32)

    ext_ref[CONV_HALO:CONV_HALO + tm, :] = proj(PW_GLUA, D_CONV) * jax.nn.sigmoid(proj(PW_GLUB, D_CONV))
    gate_ref[...] = _silu(proj(PW_GC, D_CONV))
    tail_ref[...] = ext_ref[tm:tm + CONV_HALO, :]

    def section_q():
        qb_ref[...] = (proj(PW_Q, D_ATT) * Q_PRESCALE).astype(BF16)

    def section_v():
        v = proj(PW_V, D_ATT)
        vf_ref[...] = v
        vb_ref[...] = v.astype(BF16)

    def section_ga():
        sga_ref[...] = _silu(proj(PW_GA, D_ATT))

    def section_k():
        kt = lax.dot_general(wkt_ref[...], xn, NT_DIMS, preferred_element_type=F32)
        ktf_ref[...] = kt
        ktb_ref[...] = kt.astype(BF16)

    def section_f():
        fzt = lax.dot_general(wft_ref[...], xn, NT_DIMS, preferred_element_type=F32)
        logft = jax.nn.log_sigmoid(fzt + bf_ref[...])[:N_HEADS, :]
        logft_ref[...] = logft
        lane = lax.broadcasted_iota(jnp.int32, (N_HEADS, tm), 1)
        c = logft
        shift = 1
        while shift < tm:
            c = c + jnp.where(lane >= shift, pltpu.roll(c, shift, axis=1), 0.0)
            shift *= 2
        c = c + carry_ref[:, 0:1]
        ct_ref[...] = c
        carry_ref[...] = jnp.broadcast_to(c[:, tm - 1:tm], carry_ref.shape)

    off = CONV_HALO - CONV_STATE
    span = rc + CONV_HALO - SUBLANES

    def conv_chunk(ci):
        r0 = ci * rc
        slot = ci % 2
        win = ext_ref[r0:r0 + rc + CONV_HALO, :]
        for r in range(1, SUBLANES):
            shift_ref[slot, r - 1] = win[r:r + span, :]
        acc = jnp.zeros((rc, D_CONV), F32)
        for j in range(CONV_WIDTH):
            r = (off + j) % SUBLANES
            a = off + j - r
            tap = win[a:a + rc, :] if r == 0 else shift_ref[slot, r - 1, a:a + rc, :]
            acc = acc + tap * wdw_ref[j:j + 1, :]
        mixc_ref[r0:r0 + rc, :] = _ln_swish(acc + bdw_ref[...], lg_ref, lb_ref) * gate_ref[r0:r0 + rc, :]

    sections = [section_q, section_v, section_ga, section_k, section_f]
    nchunk = tm // rc
    bounds = [nchunk * n // len(sections) for n in range(len(sections) + 1)]
    for n, section in enumerate(sections):
        section()
        for ci in range(bounds[n], bounds[n + 1]):
            conv_chunk(ci)
    ext_ref[0:CONV_HALO, :] = ext_ref[tm:tm + CONV_HALO, :]


def _proj_prompt(x, g_norm, w_tok, wkt, wft, bf_col, w_dw, b_dw, ln_g, ln_b, *, tm=512, rc=32):
    b, t, _ = x.shape
    tok = lambda d: pl.BlockSpec((None, tm, d), lambda i, j: (i, j, 0))
    tr = lambda r: pl.BlockSpec((None, r, tm), lambda i, j: (i, 0, j))
    const = lambda shape: pl.BlockSpec(shape, lambda i, j: (0,) * len(shape))
    out_shape = (
        jax.ShapeDtypeStruct((b, t, D_ATT), BF16),
        jax.ShapeDtypeStruct((b, t, D_ATT), BF16),
        jax.ShapeDtypeStruct((b, t, D_ATT), F32),
        jax.ShapeDtypeStruct((b, D_ATT, t), BF16),
        jax.ShapeDtypeStruct((b, D_ATT, t), F32),
        jax.ShapeDtypeStruct((b, N_HEADS, t), F32),
        jax.ShapeDtypeStruct((b, N_HEADS, t), F32),
        jax.ShapeDtypeStruct((b, t, D_ATT), F32),
        jax.ShapeDtypeStruct((b, t, D_CONV), F32),
        jax.ShapeDtypeStruct((b, CONV_HALO, D_CONV), F32),
    )
    out_specs = (tok(D_ATT), tok(D_ATT), tok(D_ATT), tr(D_ATT), tr(D_ATT), tr(N_HEADS), tr(N_HEADS),
                 tok(D_ATT), tok(D_CONV), pl.BlockSpec((None, CONV_HALO, D_CONV), lambda i, j: (i, 0, 0)))
    return pl.pallas_call(
        functools.partial(_proj_prompt_kernel, tm=tm, rc=rc),
        grid=(b, t // tm),
        in_specs=[tok(D_MODEL), const((1, D_MODEL)), const(w_tok.shape), const(wkt.shape), const(wft.shape),
                  const(bf_col.shape), const((CONV_WIDTH, D_CONV)), const((1, D_CONV)), const((1, D_CONV)),
                  const((1, D_CONV))],
        out_specs=out_specs,
        out_shape=out_shape,
        scratch_shapes=[pltpu.VMEM((N_HEADS, LANES), F32),
                        pltpu.VMEM((tm + CONV_HALO, D_CONV), F32),
                        pltpu.VMEM((2, SUBLANES - 1, rc + CONV_HALO - SUBLANES, D_CONV), F32),
                        pltpu.VMEM((tm, D_CONV), F32)],
        compiler_params=_cparams(("arbitrary", "arbitrary")),
        name="proj_prompt",
    )(x, g_norm, w_tok, wkt, wft, bf_col, w_dw, b_dw, ln_g, ln_b)


SW_Q, SW_K, SW_V, SW_GA, SW_GLUA, SW_GLUB, SW_GC, SW_F = 0, 512, 1024, 1536, 2048, 2560, 3072, 3584
SW_COLS = SW_F + F_PAD


def _proj_sample_kernel(x_ref, g_ref, w_ref, bf_ref, qb_ref, k_ref, v_ref, logf_ref, sga_ref, u_ref, sgc_ref):
    xn = _rms_normed(x_ref, g_ref)

    def proj(lo, width):
        return jnp.dot(xn, w_ref[:, lo:lo + width], preferred_element_type=F32)

    qb_ref[...] = (proj(SW_Q, D_ATT) * ATT_SCALE).astype(BF16)
    k_ref[...] = proj(SW_K, D_ATT)
    v_ref[...] = proj(SW_V, D_ATT)
    sga_ref[...] = _silu(proj(SW_GA, D_ATT))
    u_ref[...] = proj(SW_GLUA, D_CONV) * jax.nn.sigmoid(proj(SW_GLUB, D_CONV))
    sgc_ref[...] = _silu(proj(SW_GC, D_CONV))
    logf_ref[...] = jax.nn.log_sigmoid(proj(SW_F, F_PAD) + bf_ref[...])[:, :N_HEADS]


def _proj_sample(x, g_norm, w_tok, bf_row):
    n = x.shape[0]
    full = lambda shape: pl.BlockSpec(shape, lambda i: (0,) * len(shape))
    wide = jax.ShapeDtypeStruct((n, D_ATT), F32)
    out_shape = (jax.ShapeDtypeStruct((n, D_ATT), BF16), wide, wide, jax.ShapeDtypeStruct((n, N_HEADS), F32),
                 wide, wide, wide)
    return pl.pallas_call(
        _proj_sample_kernel,
        grid=(1,),
        in_specs=[full(x.shape), full(g_norm.shape), full(w_tok.shape), full(bf_row.shape)],
        out_specs=tuple(full(s.shape) for s in out_shape),
        out_shape=out_shape,
        compiler_params=_cparams(("arbitrary",)),
        name="proj_sample",
    )(x, g_norm, w_tok, bf_row)


def _conv_sample_kernel(st_ref, u_ref, sgc_ref, w_ref, b_ref, lg_ref, lb_ref, o_ref, ns_ref, *, ts):
    def ext(i):
        return st_ref[i] if i < CONV_STATE else u_ref[:, i - CONV_STATE, :]

    for t in range(ts):
        acc = jnp.zeros((o_ref.shape[0], D_CONV), F32)
        for j in range(CONV_WIDTH):
            acc = acc + ext(t + j) * w_ref[j:j + 1, :]
        o_ref[:, t, :] = _ln_swish(acc + b_ref[...], lg_ref, lb_ref) * sgc_ref[:, t, :]
    for i in range(CONV_STATE):
        ns_ref[i] = ext(i + ts)


def _conv_sample(state_t, u, sgc, w_dw, b_dw, ln_g, ln_b, *, bs=32):
    db, ts, _ = u.shape
    seq = pl.BlockSpec((bs, ts, D_CONV), lambda i: (i, 0, 0))
    st = pl.BlockSpec((CONV_STATE, bs, D_CONV), lambda i: (0, i, 0))
    const = lambda shape: pl.BlockSpec(shape, lambda i: (0,) * len(shape))
    return pl.pallas_call(
        functools.partial(_conv_sample_kernel, ts=ts),
        grid=(db // bs,),
        in_specs=[st, seq, seq, const((CONV_WIDTH, D_CONV)), const((1, D_CONV)), const((1, D_CONV)),
                  const((1, D_CONV))],
        out_specs=(seq, st),
        out_shape=(jax.ShapeDtypeStruct((db, ts, D_CONV), F32),
                   jax.ShapeDtypeStruct((CONV_STATE, db, D_CONV), F32)),
        compiler_params=_cparams(("arbitrary",)),
        name="conv_sample",
    )(state_t, u, sgc, w_dw, b_dw, ln_g, ln_b)


def _attn_prompt_kernel(q_ref, kt_ref, v_ref, ct_ref, o_ref, s0_ref, mblk0_ref, alpha0_ref,
                        s1_ref, mblk1_ref, alpha1_ref, m_ref, acc_ref, *, tq, rq):
    pair = pl.program_id(1)
    i = pl.program_id(2)
    q0 = pl.multiple_of(i * tq, tq)
    q2 = q_ref[...]
    lane = lax.broadcasted_iota(jnp.int32, (tq, LANES), 1)
    in_head = (lane < HEAD_DIM, lane >= HEAD_DIM)
    qh = [jnp.where(in_head[hh], q2, jnp.zeros_like(q2)) for hh in range(2)]
    head_row = lax.broadcasted_iota(jnp.int32, (N_HEADS, tq), 0)
    chains = [(hh, r0) for hh in range(2) for r0 in range(0, tq, rq)]

    def c_rows(k0):
        blk = ct_ref[:, pl.ds(k0, tq)] * LOG2E
        return [jnp.sum(jnp.where(head_row == 2 * pair + hh, blk, 0.0), axis=0, keepdims=True) for hh in range(2)]

    c_first = [r[:, 0:1] for r in c_rows(q0)]
    m_ref[...] = jnp.full_like(m_ref, NEG_BIG)
    acc_ref[...] = jnp.zeros_like(acc_ref)

    slots = ((s0_ref, mblk0_ref, alpha0_ref), (s1_ref, mblk1_ref, alpha1_ref))

    def scores(c, slot, masked, ktb, bias):
        s_ref, mblk_ref, alpha_ref = slots[slot]
        hh, r0 = chains[c]
        s = jnp.dot(qh[hh][r0:r0 + rq, :], ktb, preferred_element_type=F32) + bias[hh]
        if masked:
            row = lax.broadcasted_iota(jnp.int32, (rq, tq), 0) + r0
            col = lax.broadcasted_iota(jnp.int32, (rq, tq), 1)
            s = jnp.where(row >= col, s, NEG_BIG)
        m_prev = m_ref[c]
        m_new = jnp.maximum(m_prev, jnp.max(s, axis=1, keepdims=True))
        s_ref[c] = s
        mblk_ref[c] = m_new
        alpha_ref[c] = jnp.exp2(m_prev - m_new)
        m_ref[c] = m_new

    def accumulate(c, slot, vh):
        s_ref, mblk_ref, alpha_ref = slots[slot]
        hh, _ = chains[c]
        m_blk = mblk_ref[c]
        p = jnp.concatenate([jnp.exp2(s_ref[c, :, k * LANES:(k + 1) * LANES] - m_blk)
                             for k in range(tq // LANES)], axis=1)
        acc_ref[c] = alpha_ref[c] * acc_ref[c] + jnp.dot(p.astype(BF16), vh[hh], preferred_element_type=F32)

    def k_side(j):
        k0 = pl.multiple_of(j * tq, tq)
        c_keys = c_rows(k0)
        return kt_ref[:, pl.ds(k0, tq)], [c_first[hh] - c_keys[hh] for hh in range(2)]

    def v_side(j):
        vb = v_ref[pl.ds(pl.multiple_of(j * tq, tq), tq), :]
        return [jnp.where(in_head[hh], vb, jnp.ones_like(vb)) for hh in range(2)]

    def step(j, wslot, jprev):
        ktb, bias = k_side(j)
        vh = v_side(jprev)
        for c in range(len(chains)):
            scores(c, wslot, False, ktb, bias)
            accumulate(c, 1 - wslot, vh)

    def drain(slot, jlast):
        vh = v_side(jlast)
        for c in range(len(chains)):
            accumulate(c, slot, vh)

    ktb, bias = k_side(i)
    for c in range(len(chains)):
        scores(c, 0, True, ktb, bias)

    def two_steps(jj, carry):
        j = 2 * jj
        step(j, 1, jnp.where(jj == 0, i, j - 1))
        step(j + 1, 0, j)
        return carry

    lax.fori_loop(0, i // 2, two_steps, 0)

    @pl.when(i % 2 == 1)
    def _():
        step(i - 1, 1, jnp.where(i == 1, i, i - 2))
        drain(1, i - 1)

    @pl.when(i % 2 == 0)
    def _():
        drain(0, jnp.where(i == 0, i, i - 1))

    def head_out(hh):
        acc = jnp.concatenate([acc_ref[c] for c in range(len(chains)) if chains[c][0] == hh], axis=0)
        return acc / pltpu.roll(acc, HEAD_DIM, axis=1)

    o_ref[...] = jnp.where(in_head[0], head_out(0), head_out(1))


def _attn_prompt(qb, ktb, vb, ct, *, tq=1024, rq=256):
    b, t, _ = qb.shape
    nchain = 2 * (tq // rq)
    return pl.pallas_call(
        functools.partial(_attn_prompt_kernel, tq=tq, rq=rq),
        grid=(b, N_HEADS // 2, t // tq),
        in_specs=[pl.BlockSpec((None, tq, LANES), lambda bi, pi, qi: (bi, qi, pi)),
                  pl.BlockSpec((None, LANES, t), lambda bi, pi, qi: (bi, pi, 0)),
                  pl.BlockSpec((None, t, LANES), lambda bi, pi, qi: (bi, 0, pi)),
                  pl.BlockSpec((None, N_HEADS, t), lambda bi, pi, qi: (bi, 0, 0))],
        out_specs=pl.BlockSpec((None, tq, LANES), lambda bi, pi, qi: (bi, qi, pi)),
        out_shape=jax.ShapeDtypeStruct((b, t, D_ATT), F32),
        scratch_shapes=[pltpu.VMEM((nchain, rq, tq), F32),
                        pltpu.VMEM((nchain, rq, LANES), F32),
                        pltpu.VMEM((nchain, rq, LANES), F32)] * 2
                       + [pltpu.VMEM((nchain, rq, LANES), F32),
                          pltpu.VMEM((nchain, rq, LANES), F32)],
        compiler_params=_cparams(("arbitrary", "arbitrary", "arbitrary")),
        name="attn_prompt",
    )(qb, ktb, vb, ct)


def _suffix_matrix():
    after = (np.arange(PAGE_SIZE)[:, None] > np.arange(PAGE_SIZE)[None, :]).astype(np.float32)
    return np.concatenate([after, np.ones_like(after)], axis=1)


N_PAGE_SETS = 3


def _attn_sample_kernel(pt_ref, smat_ref, q_ref, kn_ref, vn_ref, fn_ref, f_hbm, kt_hbm, vt_hbm, o_ref,
                        fbuf, ktbuf, vtbuf, sems, qrows_ref, m_ref, l_ref, acc_ref, after_ref, *, pp, ts, ns):
    bi = pl.program_id(0)
    g = pl.program_id(1)
    n_groups = pl.num_programs(1)
    n_steps = pl.num_programs(0) * n_groups
    step = bi * n_groups + g
    ahead = N_PAGE_SETS - 1
    nrow = ts * N_HEADS

    def page_copies(s):
        b, grp, buf = s // n_groups, s % n_groups, s % N_PAGE_SETS
        copies = []
        for sq in range(ns):
            for p in range(pp):
                page = pt_ref[b * ns + sq, (n_groups - 1 - grp) * pp + p]
                for kind, (hbm, vmem) in enumerate(((f_hbm, fbuf), (kt_hbm, ktbuf), (vt_hbm, vtbuf))):
                    copies.append(pltpu.make_async_copy(hbm.at[page], vmem.at[buf, sq, p], sems.at[buf, kind]))
        return copies

    @pl.when(step == 0)
    def _():
        for s in range(ahead):
            for cp in page_copies(s):
                cp.start()

    @pl.when(step + ahead < n_steps)
    def _():
        for cp in page_copies(step + ahead):
            cp.start()

    for cp in page_copies(step):
        cp.wait()
    cur = step % N_PAGE_SETS
    row = lax.broadcasted_iota(jnp.int32, (nrow, D_ATT), 0)
    lane = lax.broadcasted_iota(jnp.int32, (nrow, D_ATT), 1)
    own_head = (lane // HEAD_DIM) == (row % N_HEADS)

    @pl.when(g == 0)
    def _():
        for sq in range(ns):
            q = q_ref[sq].astype(F32)
            qb = jnp.broadcast_to(q[:, None, :], (ts, N_HEADS, D_ATT)).reshape(nrow, D_ATT)
            qrows_ref[sq] = jnp.where(own_head, qb, 0.0)
        m_ref[...] = jnp.full_like(m_ref, NEG_BIG)
        l_ref[...] = jnp.zeros_like(l_ref)
        acc_ref[...] = jnp.zeros_like(acc_ref)
        after_ref[...] = jnp.zeros_like(after_ref)

    smat = smat_ref[...]
    for sq in range(ns):
        lhs = fbuf[cur, sq].reshape(pp * N_HEADS, PAGE_SIZE)
        hi = lhs.astype(BF16)
        r1 = lhs - hi.astype(F32)
        mid = r1.astype(BF16)
        lo = (r1 - mid.astype(F32)).astype(BF16)
        sums = (jnp.dot(hi, smat, preferred_element_type=F32) + jnp.dot(mid, smat, preferred_element_type=F32)
                + jnp.dot(lo, smat, preferred_element_type=F32))
        after = after_ref[sq]
        bias = [None] * pp
        for p in reversed(range(pp)):
            blk = sums[p * N_HEADS:(p + 1) * N_HEADS, :]
            bias[p] = blk[:, :PAGE_SIZE] + after
            after = after + blk[:, PAGE_SIZE:]
        after_ref[sq] = after

        qrows = qrows_ref[sq]
        scores = []
        for p in range(pp):
            s = jnp.dot(qrows, ktbuf[cur, sq, p], preferred_element_type=F32)
            s = s.reshape(ts, N_HEADS, PAGE_SIZE) + bias[p][None, :, :]
            scores.append(s.reshape(nrow, PAGE_SIZE))
        s_all = jnp.concatenate(scores, axis=1)
        m_prev = m_ref[sq]
        m_new = jnp.maximum(m_prev, jnp.max(s_all, axis=1, keepdims=True))
        alpha = jnp.exp(m_prev - m_new)
        prob = jnp.exp(s_all - m_new)
        l_ref[sq] = alpha * l_ref[sq] + jnp.sum(prob, axis=1, keepdims=True)
        pv = jnp.zeros((nrow, D_ATT), F32)
        for p in range(pp):
            pv = pv + lax.dot_general(prob[:, p * PAGE_SIZE:(p + 1) * PAGE_SIZE], vtbuf[cur, sq, p], NT_DIMS,
                                      preferred_element_type=F32)
        acc_ref[sq] = alpha * acc_ref[sq] + pv
        m_ref[sq] = m_new

    @pl.when(g == pl.num_programs(1) - 1)
    def _():
        for sq in range(ns):
            _attn_sample_new_keys(sq, kn_ref, vn_ref, fn_ref, o_ref, qrows_ref, m_ref, l_ref, acc_ref, own_head, ts)


def _attn_sample_new_keys(sq, kn_ref, vn_ref, fn_ref, o_ref, qrows_ref, m_ref, l_ref, acc_ref, own_head, ts):
    nrow = ts * N_HEADS
    qrows = qrows_ref[sq]
    kn = kn_ref[sq].astype(BF16).astype(F32)
    vn = vn_ref[sq].astype(BF16).astype(F32)
    fn = fn_ref[sq]
    eye = (lax.broadcasted_iota(jnp.int32, (N_HEADS, N_HEADS), 0)
           == lax.broadcasted_iota(jnp.int32, (N_HEADS, N_HEADS), 1))
    tok = lax.broadcasted_iota(jnp.int32, (nrow, 1), 0) // N_HEADS
    logits = []
    cn = jnp.zeros((1, N_HEADS), F32)
    for j in range(ts):
        cn = cn + fn[j:j + 1, :]
        cn_col = jnp.sum(jnp.where(eye, jnp.broadcast_to(cn, (N_HEADS, N_HEADS)), 0.0), axis=1, keepdims=True)
        cn_rows = jnp.broadcast_to(cn_col[None, :, :], (ts, N_HEADS, 1)).reshape(nrow, 1)
        sj = jnp.sum(qrows * kn[j:j + 1, :], axis=1, keepdims=True) - cn_rows
        logits.append(jnp.where(tok >= j, sj, NEG_BIG))
    m_prev = m_ref[sq]
    m_fin = m_prev
    for sj in logits:
        m_fin = jnp.maximum(m_fin, sj)
    alpha = jnp.exp(m_prev - m_fin)
    l_fin = alpha * l_ref[sq]
    acc = alpha * acc_ref[sq]
    for j, sj in enumerate(logits):
        pj = jnp.exp(sj - m_fin)
        l_fin = l_fin + pj
        acc = acc + pj * vn[j:j + 1, :]
    out = jnp.where(own_head, acc / l_fin, 0.0)
    o_ref[sq] = jnp.sum(out.reshape(ts, N_HEADS, D_ATT), axis=1)


def _attn_sample(q, kn, vn, fn, logf_pages_t, k_pages_t, v_pages_t, page_table, *, pp=8, ns=2):
    db, ts, _ = q.shape
    n_pages = page_table.shape[1]
    n_groups = n_pages // pp
    assert (db // ns) * n_groups >= N_PAGE_SETS, "fewer grid steps than page buffer sets"
    nrow = ts * N_HEADS
    smat = jnp.asarray(_suffix_matrix(), dtype=BF16)
    seq = lambda r, d: pl.BlockSpec((ns, r, d), lambda bi, gi, pt: (bi, 0, 0))
    in_hbm = pl.BlockSpec(memory_space=pl.ANY)
    grid_spec = pltpu.PrefetchScalarGridSpec(
        num_scalar_prefetch=1,
        grid=(db // ns, n_groups),
        in_specs=[pl.BlockSpec(smat.shape, lambda bi, gi, pt: (0, 0)),
                  seq(ts, D_ATT), seq(ts, D_ATT), seq(ts, D_ATT), seq(ts, N_HEADS), in_hbm, in_hbm, in_hbm],
        out_specs=seq(ts, D_ATT),
        scratch_shapes=[pltpu.VMEM((N_PAGE_SETS, ns, pp, N_HEADS, PAGE_SIZE), F32),
                        pltpu.VMEM((N_PAGE_SETS, ns, pp, D_ATT, PAGE_SIZE), F32),
                        pltpu.VMEM((N_PAGE_SETS, ns, pp, D_ATT, PAGE_SIZE), F32),
                        pltpu.SemaphoreType.DMA((N_PAGE_SETS, 3)),
                        pltpu.VMEM((ns, nrow, D_ATT), F32), pltpu.VMEM((ns, nrow, 1), F32),
                        pltpu.VMEM((ns, nrow, 1), F32), pltpu.VMEM((ns, nrow, D_ATT), F32),
                        pltpu.VMEM((ns, N_HEADS, PAGE_SIZE), F32)],
    )
    return pl.pallas_call(
        functools.partial(_attn_sample_kernel, pp=pp, ts=ts, ns=ns),
        grid_spec=grid_spec,
        out_shape=jax.ShapeDtypeStruct((db, ts, D_ATT), F32),
        compiler_params=_cparams(("arbitrary", "arbitrary")),
        name="attn_sample",
    )(page_table, smat, q, kn, vn, fn, logf_pages_t, k_pages_t, v_pages_t)


def _out_kernel(att_ref, sga_ref, mixc_ref, x_ref, p_ref, wo_ref, wpe_ref, gpe_ref, wpg_ref, gf_ref, y_ref):
    mix_a = (att_ref[...] * sga_ref[...]).astype(BF16)
    mix_c = mixc_ref[...].astype(BF16)
    h = (x_ref[...] + jnp.dot(mix_a, wo_ref[0:D_ATT, :], preferred_element_type=F32)
         + jnp.dot(mix_c, wo_ref[D_ATT:, :], preferred_element_type=F32))
    pe = jnp.dot(p_ref[...].astype(BF16), wpe_ref[...], preferred_element_type=F32)
    e = pe * lax.rsqrt(jnp.mean(pe * pe, axis=-1, keepdims=True) + EPS) * gpe_ref[...]
    gate = jax.nn.sigmoid(jnp.dot(h.astype(BF16), wpg_ref[...], preferred_element_type=F32))
    h = h + gate * e
    y_ref[...] = h * lax.rsqrt(jnp.mean(h * h, axis=-1, keepdims=True) + EPS) * gf_ref[...]


def _out_stage(att, sga, mixc, x, p, w_out, w_pe, g_pe, w_pg, g_final, *, tm):
    n = x.shape[0]
    tok = lambda d: pl.BlockSpec((tm, d), lambda i: (i, 0))
    const = lambda shape: pl.BlockSpec(shape, lambda i: (0,) * len(shape))
    return pl.pallas_call(
        _out_kernel,
        grid=(n // tm,),
        in_specs=[tok(D_ATT), tok(D_ATT), tok(D_CONV), tok(D_MODEL), tok(D_PLE),
                  const((D_ATT + D_CONV, D_MODEL)), const((D_PLE, D_MODEL)), const((1, D_MODEL)),
                  const((D_MODEL, D_MODEL)), const((1, D_MODEL))],
        out_specs=tok(D_MODEL),
        out_shape=jax.ShapeDtypeStruct((n, D_MODEL), F32),
        compiler_params=_cparams(("arbitrary",)),
        name="out_stage",
    )(att, sga, mixc, x, p, w_out, w_pe, g_pe, w_pg, g_final)


def _in_weights(w_in, b_f):
    a, c, h = D_ATT, D_CONV, N_HEADS
    w = w_in.astype(BF16)
    q, k, v = w[:, 0:a], w[:, a:2 * a], w[:, 2 * a:3 * a]
    f = w[:, 3 * a:3 * a + h]
    o = 3 * a + h
    rest = w[:, o:]
    w_prompt = jnp.concatenate([q, v, rest], axis=1)
    wkt = k.T
    wft = jnp.pad(f.T, ((0, BF16_SUBLANES - h), (0, 0)))
    bf_col = jnp.pad(b_f, (0, BF16_SUBLANES - h)).reshape(BF16_SUBLANES, 1)
    w_sample = jnp.concatenate([q, k, v, rest, jnp.pad(f, ((0, 0), (0, F_PAD - h)))], axis=1)
    bf_row = jnp.pad(b_f, (0, F_PAD - h)).reshape(1, F_PAD)
    return w_prompt, wkt, wft, bf_col, w_sample, bf_row


def kernel(x_prompt, x_sample, cache_k, cache_v, cache_logf, state_conv, page_table, p_prompt, p_sample,
           g_norm, w_in, b_f, w_dw, b_dw, ln_g, ln_b, w_out, w_pe, g_pe, w_pg, g_final):
    assert w_in.shape[0] == 1, "single-layer step"
    b, t, _ = x_prompt.shape
    db, ts, _ = x_sample.shape
    n_phys = cache_k.shape[1]
    row = lambda a: a.reshape(1, -1)

    w_prompt, wkt, wft, bf_col, w_sample, bf_row = _in_weights(w_in[0], b_f[0])
    wo, wpe, wpg = w_out[0].astype(BF16), w_pe[0].astype(BF16), w_pg[0].astype(BF16)
    gn, gpe, gf = row(g_norm[0]), row(g_pe[0]), row(g_final)
    bdw, lng, lnb = row(b_dw[0]), row(ln_g[0]), row(ln_b[0])

    qb, vb, v_p, ktb, kt_p, logft_p, ct, sga_p, mixc_p, u_tail = _proj_prompt(
        x_prompt, gn, w_prompt, wkt, wft, bf_col, w_dw[0], bdw, lng, lnb)
    att_p = _attn_prompt(qb, ktb, vb, ct)
    n_p = b * t
    y_p = _out_stage(att_p.reshape(n_p, D_ATT), sga_p.reshape(n_p, D_ATT), mixc_p.reshape(n_p, D_CONV),
                     x_prompt.reshape(n_p, D_MODEL), p_prompt[0].reshape(n_p, D_PLE),
                     wo, wpe, gpe, wpg, gf, tm=512).reshape(b, t, D_MODEL)
    k_prompt = jnp.transpose(kt_p.reshape(b, N_HEADS, HEAD_DIM, t), (0, 3, 1, 2))[None]
    v_prompt = v_p.reshape(1, b, t, N_HEADS, HEAD_DIM)
    logf_prompt = jnp.transpose(logft_p, (0, 2, 1))[None]
    conv_prompt = u_tail[:, CONV_HALO - CONV_STATE:, :][None]

    n_s = db * ts
    qb_s, k_s, v_s, logf_s, sga_s, u_s, sgc_s = _proj_sample(x_sample.reshape(n_s, D_MODEL), gn, w_sample, bf_row)
    seqs = lambda a: a.reshape(db, ts, a.shape[-1])
    state_t = jnp.transpose(state_conv[0], (1, 0, 2))
    mixc_s, new_state_t = _conv_sample(state_t, seqs(u_s), seqs(sgc_s), w_dw[0], bdw, lng, lnb)
    k_pages_t = jnp.transpose(cache_k[0], (0, 2, 3, 1)).reshape(n_phys, D_ATT, PAGE_SIZE)
    v_pages_t = jnp.transpose(cache_v[0], (0, 2, 3, 1)).reshape(n_phys, D_ATT, PAGE_SIZE)
    logf_pages_t = jnp.transpose(cache_logf[0], (0, 2, 1))
    att_s = _attn_sample(seqs(qb_s), seqs(k_s), seqs(v_s), seqs(logf_s), logf_pages_t, k_pages_t, v_pages_t,
                         page_table)
    y_s = _out_stage(att_s.reshape(n_s, D_ATT), sga_s, mixc_s.reshape(n_s, D_CONV),
                     x_sample.reshape(n_s, D_MODEL), p_sample[0].reshape(n_s, D_PLE),
                     wo, wpe, gpe, wpg, gf, tm=n_s).reshape(db, ts, D_MODEL)
    conv_sample = jnp.transpose(new_state_t, (1, 0, 2))[None]

    heads = lambda a: a.reshape(1, db, ts, N_HEADS, HEAD_DIM)
    return (y_p, y_s, k_prompt, v_prompt, logf_prompt, conv_prompt,
            heads(k_s), heads(v_s), logf_s.reshape(1, db, ts, N_HEADS), conv_sample)
```
